```python
import jax, jax.numpy as jnp
from jax import lax
import numpy as np

D_MODEL = 2048
BATCH = 8
SEQ = 8192
DEPTH = 1

N_MEM = 256
MLA_HEADS = 16
QK_NOPE_DIM = 128
QK_ROPE_DIM = 64
V_HEAD_DIM = 128
KV_LORA_RANK = 512
QK_HEAD_DIM = QK_NOPE_DIM + QK_ROPE_DIM
SOFTMAX_SCALE = QK_HEAD_DIM ** -0.5
ROPE_THETA = 10000.0
Q_BLOCK = 128
LRU_WIDTH = D_MODEL
LRU_BLOCKS = 16
LRU_BLOCK_DIM = LRU_WIDTH // LRU_BLOCKS
LRU_CONV_WIDTH = 4
LRU_C = 8.0
X_HEADS = 4
X_HEAD_DIM = 128
D_FF = 5632
FFN_CONV_WIDTH = 3
EPS = 1e-6

Q_COLS = MLA_HEADS * QK_HEAD_DIM
IN_SIZES = (Q_COLS, KV_LORA_RANK, QK_ROPE_DIM, LRU_WIDTH, LRU_WIDTH, D_MODEL, D_MODEL)
IN_COLS = sum(IN_SIZES)
IN_SPLIT_IDX = tuple(int(v) for v in np.cumsum(IN_SIZES)[:-1])

kernel_name = "hybrid_mla_rglru_gated_merge_layer"


def rmsnorm(x, g):
    x32 = x.astype(jnp.float32)
    y = x32 * lax.rsqrt(jnp.mean(x32 * x32, axis=-1, keepdims=True) + EPS)
    return (y * g.astype(jnp.float32)).astype(x.dtype)


def rope_tables(positions):
    inv_freq = ROPE_THETA ** (-jnp.arange(0, QK_ROPE_DIM, 2, dtype=jnp.float32) / QK_ROPE_DIM)
    ang = positions.astype(jnp.float32)[..., None] * inv_freq
    return jnp.cos(ang), jnp.sin(ang)


def apply_rope(x, cos, sin):
    x32 = x.astype(jnp.float32)
    x1, x2 = jnp.split(x32, 2, axis=-1)
    out = jnp.concatenate([x1 * cos - x2 * sin, x2 * cos + x1 * sin], axis=-1)
    return out.astype(x.dtype)


def causal_dwconv(x, w, b):
    k_width = w.shape[0]
    s = x.shape[1]
    xp = jnp.pad(x, ((0, 0), (k_width - 1, 0), (0, 0)))
    acc = b
    for k in range(k_width):
        acc = acc + xp[:, k:k + s] * w[k]
    return acc


def mla_attention(q_in, ckv, k_rope, positions, g_ckv, w_ukv):
    b, s, _ = q_in.shape
    cos, sin = rope_tables(positions)
    q = q_in.reshape(b, s, MLA_HEADS, QK_HEAD_DIM)
    q_nope, q_rope = q[..., :QK_NOPE_DIM], q[..., QK_NOPE_DIM:]
    q_rope = apply_rope(q_rope, cos[:, :, None], sin[:, :, None])
    q = jnp.concatenate([q_nope, q_rope], axis=-1)
    k_rope = apply_rope(k_rope, cos, sin)
    kv = (rmsnorm(ckv, g_ckv) @ w_ukv).reshape(b, s, MLA_HEADS, QK_NOPE_DIM + V_HEAD_DIM)
    k_nope, v = kv[..., :QK_NOPE_DIM], kv[..., QK_NOPE_DIM:]
    k = jnp.concatenate([k_nope, jnp.broadcast_to(k_rope[:, :, None], (b, s, MLA_HEADS, QK_ROPE_DIM))], axis=-1)
    n_blocks = s // Q_BLOCK
    q_blocks = q.reshape(b, n_blocks, Q_BLOCK, MLA_HEADS, QK_HEAD_DIM).transpose(1, 0, 2, 3, 4)
    starts = jnp.arange(n_blocks, dtype=jnp.int32) * Q_BLOCK
    k_idx = jnp.arange(s, dtype=jnp.int32)

    def one_block(args):
        qb, start = args
        sc = jnp.einsum('bqhd,bkhd->bhqk', qb, k).astype(jnp.float32) * SOFTMAX_SCALE
        q_idx = start + jnp.arange(Q_BLOCK, dtype=jnp.int32)
        mask = q_idx[:, None] >= k_idx[None, :]
        p = jax.nn.softmax(jnp.where(mask, sc, -jnp.inf), axis=-1)
        return jnp.einsum('bhqk,bkhd->bqhd', p.astype(v.dtype), v)

    o = lax.map(one_block, (q_blocks, starts))
    return o.transpose(1, 0, 2, 3, 4).reshape(b, s, MLA_HEADS * V_HEAD_DIM)


def rg_lru(x, w_rg, b_rg, w_ig, b_ig, lam):
    b, s, w = x.shape
    xh = x.reshape(b, s, LRU_BLOCKS, LRU_BLOCK_DIM)
    r = jax.nn.sigmoid(jnp.einsum('bshi,hij->bshj', xh, w_rg) + b_rg).reshape(b, s, w)
    i = jax.nn.sigmoid(jnp.einsum('bshi,hij->bshj', xh, w_ig) + b_ig).reshape(b, s, w)
    log_a = -LRU_C * r.astype(jnp.float32) * jax.nn.softplus(-lam.astype(jnp.float32))
    a = jnp.exp(log_a)
    u = jnp.sqrt(-jnp.expm1(2.0 * log_a)) * (i * x).astype(jnp.float32)

    def combine(left, right):
        a1, b1 = left
        a2, b2 = right
        return a1 * a2, a2 * b1 + b2

    _, h = lax.associative_scan(combine, (a, u), axis=1)
    return h.astype(x.dtype)


def _fwd_setup_inputs(seed: int = 0) -> dict:
    key = jax.random.key(seed)
    ks = jax.random.split(key, 32)
    f32 = jnp.float32

    def nrm(k, shape, fan_in):
        return jax.random.normal(k, shape, f32) * (fan_in ** -0.5)

    def gain(k, shape):
        return 1.0 + 0.05 * jax.random.normal(k, shape, f32)

    def bias(k, shape):
        return 0.01 * jax.random.normal(k, shape, f32)

    L = DEPTH
    x = jax.random.normal(ks[0], (BATCH, SEQ, D_MODEL), f32)
    mem = jax.random.normal(ks[1], (BATCH, N_MEM, D_MODEL), f32)
    offset = jax.random.randint(ks[2], (BATCH, 1), 0, 1024, dtype=jnp.int32)
    positions = offset + jnp.arange(SEQ, dtype=jnp.int32)[None, :]
    a0 = jax.random.uniform(ks[3], (L, LRU_WIDTH), f32, 0.9, 0.999)
    sig = a0 ** (1.0 / LRU_C)
    lru_lambda = jnp.log(sig) - jnp.log1p(-sig)
    return {
        "x": x,
        "mem": mem,
        "positions": positions,
        "g_pre_mix": gain(ks[4], (L, D_MODEL)),
        "g_post_mix": gain(ks[5], (L, D_MODEL)),
        "w_in": nrm(ks[6], (L, D_MODEL, IN_COLS), D_MODEL),
        "g_ckv": gain(ks[7], (L, KV_LORA_RANK)),
        "w_ukv": nrm(ks[8], (L, KV_LORA_RANK, MLA_HEADS * (QK_NOPE_DIM + V_HEAD_DIM)), KV_LORA_RANK),
        "w_o_mla": nrm(ks[9], (L, MLA_HEADS * V_HEAD_DIM, D_MODEL), MLA_HEADS * V_HEAD_DIM),
        "w_conv_lru": nrm(ks[10], (L, LRU_CONV_WIDTH, LRU_WIDTH), LRU_CONV_WIDTH),
        "b_conv_lru": bias(ks[11], (L, LRU_WIDTH)),
        "w_rg": nrm(ks[12], (L, LRU_BLOCKS, LRU_BLOCK_DIM, LRU_BLOCK_DIM), LRU_BLOCK_DIM),
        "b_rg": bias(ks[13], (L, LRU_BLOCKS, LRU_BLOCK_DIM)),
        "w_ig": nrm(ks[14], (L, LRU_BLOCKS, LRU_BLOCK_DIM, LRU_BLOCK_DIM), LRU_BLOCK_DIM),
        "b_ig": bias(ks[15], (L, LRU_BLOCKS, LRU_BLOCK_DIM)),
        "lru_lambda": lru_lambda,
        "w_o_lru": nrm(ks[16], (L, LRU_WIDTH, D_MODEL), LRU_WIDTH),
        "w_out": nrm(ks[17], (L, D_MODEL, D_MODEL), D_MODEL),
        "g_pre_x": gain(ks[18], (L, D_MODEL)),
        "g_post_x": gain(ks[19], (L, D_MODEL)),
        "g_mem": gain(ks[20], (L, D_MODEL)),
        "w_cq": nrm(ks[21], (L, D_MODEL, X_HEADS * X_HEAD_DIM), D_MODEL),
        "w_ck": nrm(ks[22], (L, D_MODEL, X_HEADS * X_HEAD_DIM), D_MODEL),
        "w_cv": nrm(ks[23], (L, D_MODEL, X_HEADS * X_HEAD_DIM), D_MODEL),
        "w_co": nrm(ks[24], (L, X_HEADS * X_HEAD_DIM, D_MODEL), X_HEADS * X_HEAD_DIM),
        "g_pre_ffn": gain(ks[25], (L, D_MODEL)),
        "g_post_ffn": gain(ks[26], (L, D_MODEL)),
        "w_up": nrm(ks[27], (L, D_MODEL, 2 * D_FF), D_MODEL),
        "w_fconv": nrm(ks[28], (L, FFN_CONV_WIDTH, 2 * D_FF), FFN_CONV_WIDTH),
        "b_fconv": bias(ks[29], (L, 2 * D_FF)),
        "w_down": nrm(ks[30], (L, D_FF, D_MODEL), D_FF),
    }


def _fwd_reference(x, mem, positions, g_pre_mix, g_post_mix, w_in, g_ckv, w_ukv, w_o_mla,
              w_conv_lru, b_conv_lru, w_rg, b_rg, w_ig, b_ig, lru_lambda, w_o_lru, w_out,
              g_pre_x, g_post_x, g_mem, w_cq, w_ck, w_cv, w_co,
              g_pre_ffn, g_post_ffn, w_up, w_fconv, b_fconv, w_down):
    b, s, _ = x.shape
    m = mem.shape[1]
    for l in range(DEPTH):
        h = rmsnorm(x, g_pre_mix[l])
        proj = h @ w_in[l]
        q_in, ckv, k_rope, lru_x, lru_y, gate_mla, gate_lru = jnp.split(proj, IN_SPLIT_IDX, axis=-1)
        o_mla = mla_attention(q_in, ckv, k_rope, positions, g_ckv[l], w_ukv[l]) @ w_o_mla[l]
        xc = causal_dwconv(lru_x, w_conv_lru[l], b_conv_lru[l])
        hr = rg_lru(xc, w_rg[l], b_rg[l], w_ig[l], b_ig[l], lru_lambda[l])
        o_lru = (hr * jax.nn.gelu(lru_y, approximate=True)) @ w_o_lru[l]
        merged = jax.nn.sigmoid(gate_mla) * o_mla + jax.nn.sigmoid(gate_lru) * o_lru
        x = x + rmsnorm(merged @ w_out[l], g_post_mix[l])
        h = rmsnorm(x, g_pre_x[l])
        mn = rmsnorm(mem, g_mem[l])
        cq = (h @ w_cq[l]).reshape(b, s, X_HEADS, X_HEAD_DIM)
        ck = (mn @ w_ck[l]).reshape(b, m, X_HEADS, X_HEAD_DIM)
        cv = (mn @ w_cv[l]).reshape(b, m, X_HEADS, X_HEAD_DIM)
        sc = jnp.einsum('bshd,bmhd->bhsm', cq, ck).astype(jnp.float32) * (X_HEAD_DIM ** -0.5)
        p = jax.nn.softmax(sc, axis=-1).astype(cv.dtype)
        co = jnp.einsum('bhsm,bmhd->bshd', p, cv).reshape(b, s, X_HEADS * X_HEAD_DIM)
        x = x + rmsnorm(co @ w_co[l], g_post_x[l])
        h = rmsnorm(x, g_pre_ffn[l])
        up = causal_dwconv(h @ w_up[l], w_fconv[l], b_fconv[l])
        gate, val = jnp.split(up, 2, axis=-1)
        x = x + rmsnorm((jax.nn.gelu(gate, approximate=True) * val) @ w_down[l], g_post_ffn[l])
    return x


import jax as _jax
import jax.numpy as _jnp

TWIN_FORMAT = 'train_step'
FWD_PARAMS = ['x', 'mem', 'positions', 'g_pre_mix', 'g_post_mix', 'w_in', 'g_ckv', 'w_ukv', 'w_o_mla', 'w_conv_lru', 'b_conv_lru', 'w_rg', 'b_rg', 'w_ig', 'b_ig', 'lru_lambda', 'w_o_lru', 'w_out', 'g_pre_x', 'g_post_x', 'g_mem', 'w_cq', 'w_ck', 'w_cv', 'w_co', 'g_pre_ffn', 'g_post_ffn', 'w_up', 'w_fconv', 'b_fconv', 'w_down']
TWIN_WEIGHTS = ['g_pre_mix', 'g_post_mix', 'w_in', 'g_ckv', 'w_ukv', 'w_o_mla', 'w_conv_lru', 'b_conv_lru', 'w_rg', 'b_rg', 'w_ig', 'b_ig', 'lru_lambda', 'w_o_lru', 'w_out', 'g_pre_x', 'g_post_x', 'g_mem', 'w_cq', 'w_ck', 'w_cv', 'w_co', 'g_pre_ffn', 'g_post_ffn', 'w_up', 'w_fconv', 'b_fconv', 'w_down']
TWIN_DIFF_INPUT = 'x'
TWIN_INPUTS = ['x', 'mem', 'positions', 'g_pre_mix', 'g_post_mix', 'w_in', 'g_ckv', 'w_ukv', 'w_o_mla', 'w_conv_lru', 'b_conv_lru', 'w_rg', 'b_rg', 'w_ig', 'b_ig', 'lru_lambda', 'w_o_lru', 'w_out', 'g_pre_x', 'g_post_x', 'g_mem', 'w_cq', 'w_ck', 'w_cv', 'w_co', 'g_pre_ffn', 'g_post_ffn', 'w_up', 'w_fconv', 'b_fconv', 'w_down', 'loss_target', 'm_g_pre_mix', 'm_g_post_mix', 'm_w_in', 'm_g_ckv', 'm_w_ukv', 'm_w_o_mla', 'm_w_conv_lru', 'm_b_conv_lru', 'm_w_rg', 'm_b_rg', 'm_w_ig', 'm_b_ig', 'm_lru_lambda', 'm_w_o_lru', 'm_w_out', 'm_g_pre_x', 'm_g_post_x', 'm_g_mem', 'm_w_cq', 'm_w_ck', 'm_w_cv', 'm_w_co', 'm_g_pre_ffn', 'm_g_post_ffn', 'm_w_up', 'm_w_fconv', 'm_b_fconv', 'm_w_down', 'v_g_pre_mix', 'v_g_post_mix', 'v_w_in', 'v_g_ckv', 'v_w_ukv', 'v_w_o_mla', 'v_w_conv_lru', 'v_b_conv_lru', 'v_w_rg', 'v_b_rg', 'v_w_ig', 'v_b_ig', 'v_lru_lambda', 'v_w_o_lru', 'v_w_out', 'v_g_pre_x', 'v_g_post_x', 'v_g_mem', 'v_w_cq', 'v_w_ck', 'v_w_cv', 'v_w_co', 'v_g_pre_ffn', 'v_g_post_ffn', 'v_w_up', 'v_w_fconv', 'v_b_fconv', 'v_w_down']
TWIN_OUTPUTS = ['loss', 'grad_x', 'grad_g_pre_mix', 'grad_g_post_mix', 'grad_w_in', 'grad_g_ckv', 'grad_w_ukv', 'grad_w_o_mla', 'grad_w_conv_lru', 'grad_b_conv_lru', 'grad_w_rg', 'grad_b_rg', 'grad_w_ig', 'grad_b_ig', 'grad_lru_lambda', 'grad_w_o_lru', 'grad_w_out', 'grad_g_pre_x', 'grad_g_post_x', 'grad_g_mem', 'grad_w_cq', 'grad_w_ck', 'grad_w_cv', 'grad_w_co', 'grad_g_pre_ffn', 'grad_g_post_ffn', 'grad_w_up', 'grad_w_fconv', 'grad_b_fconv', 'grad_w_down', 'delta_g_pre_mix', 'delta_g_post_mix', 'delta_w_in', 'delta_g_ckv', 'delta_w_ukv', 'delta_w_o_mla', 'delta_w_conv_lru', 'delta_b_conv_lru', 'delta_w_rg', 'delta_b_rg', 'delta_w_ig', 'delta_b_ig', 'delta_lru_lambda', 'delta_w_o_lru', 'delta_w_out', 'delta_g_pre_x', 'delta_g_post_x', 'delta_g_mem', 'delta_w_cq', 'delta_w_ck', 'delta_w_cv', 'delta_w_co', 'delta_g_pre_ffn', 'delta_g_post_ffn', 'delta_w_up', 'delta_w_fconv', 'delta_b_fconv', 'delta_w_down', 'new_m_g_pre_mix', 'new_m_g_post_mix', 'new_m_w_in', 'new_m_g_ckv', 'new_m_w_ukv', 'new_m_w_o_mla', 'new_m_w_conv_lru', 'new_m_b_conv_lru', 'new_m_w_rg', 'new_m_b_rg', 'new_m_w_ig', 'new_m_b_ig', 'new_m_lru_lambda', 'new_m_w_o_lru', 'new_m_w_out', 'new_m_g_pre_x', 'new_m_g_post_x', 'new_m_g_mem', 'new_m_w_cq', 'new_m_w_ck', 'new_m_w_cv', 'new_m_w_co', 'new_m_g_pre_ffn', 'new_m_g_post_ffn', 'new_m_w_up', 'new_m_w_fconv', 'new_m_b_fconv', 'new_m_w_down', 'new_v_g_pre_mix', 'new_v_g_post_mix', 'new_v_w_in', 'new_v_g_ckv', 'new_v_w_ukv', 'new_v_w_o_mla', 'new_v_w_conv_lru', 'new_v_b_conv_lru', 'new_v_w_rg', 'new_v_b_rg', 'new_v_w_ig', 'new_v_b_ig', 'new_v_lru_lambda', 'new_v_w_o_lru', 'new_v_w_out', 'new_v_g_pre_x', 'new_v_g_post_x', 'new_v_g_mem', 'new_v_w_cq', 'new_v_w_ck', 'new_v_w_cv', 'new_v_w_co', 'new_v_g_pre_ffn', 'new_v_g_post_ffn', 'new_v_w_up', 'new_v_w_fconv', 'new_v_b_fconv', 'new_v_w_down']
TWIN_LEAF_KINDS = {'loss': 'loss', 'grad_x': 'grad_x', 'grad_g_pre_mix': 'grad_w', 'grad_g_post_mix': 'grad_w', 'grad_w_in': 'grad_w', 'grad_g_ckv': 'grad_w', 'grad_w_ukv': 'grad_w', 'grad_w_o_mla': 'grad_w', 'grad_w_conv_lru': 'grad_w', 'grad_b_conv_lru': 'grad_w', 'grad_w_rg': 'grad_w', 'grad_b_rg': 'grad_w', 'grad_w_ig': 'grad_w', 'grad_b_ig': 'grad_w', 'grad_lru_lambda': 'grad_w', 'grad_w_o_lru': 'grad_w', 'grad_w_out': 'grad_w', 'grad_g_pre_x': 'grad_w', 'grad_g_post_x': 'grad_w', 'grad_g_mem': 'grad_w', 'grad_w_cq': 'grad_w', 'grad_w_ck': 'grad_w', 'grad_w_cv': 'grad_w', 'grad_w_co': 'grad_w', 'grad_g_pre_ffn': 'grad_w', 'grad_g_post_ffn': 'grad_w', 'grad_w_up': 'grad_w', 'grad_w_fconv': 'grad_w', 'grad_b_fconv': 'grad_w', 'grad_w_down': 'grad_w', 'delta_g_pre_mix': 'delta_w', 'delta_g_post_mix': 'delta_w', 'delta_w_in': 'delta_w', 'delta_g_ckv': 'delta_w', 'delta_w_ukv': 'delta_w', 'delta_w_o_mla': 'delta_w', 'delta_w_conv_lru': 'delta_w', 'delta_b_conv_lru': 'delta_w', 'delta_w_rg': 'delta_w', 'delta_b_rg': 'delta_w', 'delta_w_ig': 'delta_w', 'delta_b_ig': 'delta_w', 'delta_lru_lambda': 'delta_w', 'delta_w_o_lru': 'delta_w', 'delta_w_out': 'delta_w', 'delta_g_pre_x': 'delta_w', 'delta_g_post_x': 'delta_w', 'delta_g_mem': 'delta_w', 'delta_w_cq': 'delta_w', 'delta_w_ck': 'delta_w', 'delta_w_cv': 'delta_w', 'delta_w_co': 'delta_w', 'delta_g_pre_ffn': 'delta_w', 'delta_g_post_ffn': 'delta_w', 'delta_w_up': 'delta_w', 'delta_w_fconv': 'delta_w', 'delta_b_fconv': 'delta_w', 'delta_w_down': 'delta_w', 'new_m_g_pre_mix': 'new_m', 'new_m_g_post_mix': 'new_m', 'new_m_w_in': 'new_m', 'new_m_g_ckv': 'new_m', 'new_m_w_ukv': 'new_m', 'new_m_w_o_mla': 'new_m', 'new_m_w_conv_lru': 'new_m', 'new_m_b_conv_lru': 'new_m', 'new_m_w_rg': 'new_m', 'new_m_b_rg': 'new_m', 'new_m_w_ig': 'new_m', 'new_m_b_ig': 'new_m', 'new_m_lru_lambda': 'new_m', 'new_m_w_o_lru': 'new_m', 'new_m_w_out': 'new_m', 'new_m_g_pre_x': 'new_m', 'new_m_g_post_x': 'new_m', 'new_m_g_mem': 'new_m', 'new_m_w_cq': 'new_m', 'new_m_w_ck': 'new_m', 'new_m_w_cv': 'new_m', 'new_m_w_co': 'new_m', 'new_m_g_pre_ffn': 'new_m', 'new_m_g_post_ffn': 'new_m', 'new_m_w_up': 'new_m', 'new_m_w_fconv': 'new_m', 'new_m_b_fconv': 'new_m', 'new_m_w_down': 'new_m', 'new_v_g_pre_mix': 'new_v', 'new_v_g_post_mix': 'new_v', 'new_v_w_in': 'new_v', 'new_v_g_ckv': 'new_v', 'new_v_w_ukv': 'new_v', 'new_v_w_o_mla': 'new_v', 'new_v_w_conv_lru': 'new_v', 'new_v_b_conv_lru': 'new_v', 'new_v_w_rg': 'new_v', 'new_v_b_rg': 'new_v', 'new_v_w_ig': 'new_v', 'new_v_b_ig': 'new_v', 'new_v_lru_lambda': 'new_v', 'new_v_w_o_lru': 'new_v', 'new_v_w_out': 'new_v', 'new_v_g_pre_x': 'new_v', 'new_v_g_post_x': 'new_v', 'new_v_g_mem': 'new_v', 'new_v_w_cq': 'new_v', 'new_v_w_ck': 'new_v', 'new_v_w_cv': 'new_v', 'new_v_w_co': 'new_v', 'new_v_g_pre_ffn': 'new_v', 'new_v_g_post_ffn': 'new_v', 'new_v_w_up': 'new_v', 'new_v_w_fconv': 'new_v', 'new_v_b_fconv': 'new_v', 'new_v_w_down': 'new_v'}


def _forward(args):
    return _fwd_reference(*[args[k] for k in FWD_PARAMS])


def _output_shape():
    def fwd():
        inp = _fwd_setup_inputs(0)
        return _fwd_reference(*[inp[k] for k in FWD_PARAMS])
    out = _jax.eval_shape(fwd)
    return out.shape, out.dtype

N_MICROBATCH = 1
ADAM_LR = 0.001
ADAM_B1 = 0.9
ADAM_B2 = 0.999
ADAM_EPS = 1e-08
ADAM_WD = 0.01
ADAM_STEP = 10
PER_EXAMPLE_BATCH_AXIS = {'x': 0, 'mem': 0, 'positions': 0, 'loss_target': 0}
SHARED_INPUTS = []
_WEIGHT_DTYPES = {'g_pre_mix': _jnp.float32, 'g_post_mix': _jnp.float32, 'w_in': _jnp.float32, 'g_ckv': _jnp.float32, 'w_ukv': _jnp.float32, 'w_o_mla': _jnp.float32, 'w_conv_lru': _jnp.float32, 'b_conv_lru': _jnp.float32, 'w_rg': _jnp.float32, 'b_rg': _jnp.float32, 'w_ig': _jnp.float32, 'b_ig': _jnp.float32, 'lru_lambda': _jnp.float32, 'w_o_lru': _jnp.float32, 'w_out': _jnp.float32, 'g_pre_x': _jnp.float32, 'g_post_x': _jnp.float32, 'g_mem': _jnp.float32, 'w_cq': _jnp.float32, 'w_ck': _jnp.float32, 'w_cv': _jnp.float32, 'w_co': _jnp.float32, 'g_pre_ffn': _jnp.float32, 'g_post_ffn': _jnp.float32, 'w_up': _jnp.float32, 'w_fconv': _jnp.float32, 'b_fconv': _jnp.float32, 'w_down': _jnp.float32}
MOMENT_SCALE = {'g_pre_mix': 8.164521e-01, 'g_post_mix': 3.190852e+01, 'w_in': 3.391416e-01, 'g_ckv': 7.790879e-01, 'w_ukv': 2.684048e-01, 'w_o_mla': 3.536939e-01, 'w_conv_lru': 1.078353e+00, 'b_conv_lru': 2.498836e+01, 'w_rg': 6.460811e-01, 'b_rg': 4.234392e-01, 'w_ig': 1.223959e+00, 'b_ig': 3.265342e-01, 'lru_lambda': 6.210790e-01, 'w_o_lru': 1.310608e+00, 'w_out': 1.179556e+00, 'g_pre_x': 4.614711e-01, 'g_post_x': 3.260239e+01, 'g_mem': 2.011587e+00, 'w_cq': 9.520511e-01, 'w_ck': 9.610719e-01, 'w_cv': 3.936324e+00, 'w_co': 2.000725e+00, 'g_pre_ffn': 1.251697e+00, 'g_post_ffn': 3.198357e+01, 'w_up': 5.433915e-01, 'w_fconv': 6.296003e-01, 'b_fconv': 2.033303e+00, 'w_down': 1.193666e+00}


def _to_microbatches(a, axis):
    t = _jnp.moveaxis(a, axis, 0)
    t = t.reshape((N_MICROBATCH, t.shape[0] // N_MICROBATCH) + t.shape[1:])
    return _jnp.moveaxis(t, 1, axis + 1)


def setup_inputs(seed: int = 0) -> dict:
    inp = _fwd_setup_inputs(seed)
    key = _jax.random.fold_in(_jax.random.key(seed), 7919)
    shape, _ = _output_shape()
    out = dict(inp)
    out["loss_target"] = _jax.random.normal(_jax.random.fold_in(key, 0), shape, _jnp.float32)
    for i, name in enumerate(TWIN_WEIGHTS):
        w = inp[name].astype(_jnp.float32)
        if MOMENT_SCALE is None:
            s = _jnp.sqrt(_jnp.mean(_jnp.square(w)) + 1e-30)
        else:
            s = MOMENT_SCALE[name]
        km, kv = _jax.random.split(_jax.random.fold_in(key, i + 1))
        out[name] = w
        out["m_" + name] = s * _jax.random.normal(km, w.shape, _jnp.float32)
        out["v_" + name] = (s * s) * _jax.random.uniform(kv, w.shape, _jnp.float32, 0.5, 1.5)
    if N_MICROBATCH > 1:
        for name, axis in PER_EXAMPLE_BATCH_AXIS.items():
            out[name] = _to_microbatches(out[name], axis)
    return {'x': out['x'], 'mem': out['mem'], 'positions': out['positions'], 'g_pre_mix': out['g_pre_mix'], 'g_post_mix': out['g_post_mix'], 'w_in': out['w_in'], 'g_ckv': out['g_ckv'], 'w_ukv': out['w_ukv'], 'w_o_mla': out['w_o_mla'], 'w_conv_lru': out['w_conv_lru'], 'b_conv_lru': out['b_conv_lru'], 'w_rg': out['w_rg'], 'b_rg': out['b_rg'], 'w_ig': out['w_ig'], 'b_ig': out['b_ig'], 'lru_lambda': out['lru_lambda'], 'w_o_lru': out['w_o_lru'], 'w_out': out['w_out'], 'g_pre_x': out['g_pre_x'], 'g_post_x': out['g_post_x'], 'g_mem': out['g_mem'], 'w_cq': out['w_cq'], 'w_ck': out['w_ck'], 'w_cv': out['w_cv'], 'w_co': out['w_co'], 'g_pre_ffn': out['g_pre_ffn'], 'g_post_ffn': out['g_post_ffn'], 'w_up': out['w_up'], 'w_fconv': out['w_fconv'], 'b_fconv': out['b_fconv'], 'w_down': out['w_down'], 'loss_target': out['loss_target'], 'm_g_pre_mix': out['m_g_pre_mix'], 'm_g_post_mix': out['m_g_post_mix'], 'm_w_in': out['m_w_in'], 'm_g_ckv': out['m_g_ckv'], 'm_w_ukv': out['m_w_ukv'], 'm_w_o_mla': out['m_w_o_mla'], 'm_w_conv_lru': out['m_w_conv_lru'], 'm_b_conv_lru': out['m_b_conv_lru'], 'm_w_rg': out['m_w_rg'], 'm_b_rg': out['m_b_rg'], 'm_w_ig': out['m_w_ig'], 'm_b_ig': out['m_b_ig'], 'm_lru_lambda': out['m_lru_lambda'], 'm_w_o_lru': out['m_w_o_lru'], 'm_w_out': out['m_w_out'], 'm_g_pre_x': out['m_g_pre_x'], 'm_g_post_x': out['m_g_post_x'], 'm_g_mem': out['m_g_mem'], 'm_w_cq': out['m_w_cq'], 'm_w_ck': out['m_w_ck'], 'm_w_cv': out['m_w_cv'], 'm_w_co': out['m_w_co'], 'm_g_pre_ffn': out['m_g_pre_ffn'], 'm_g_post_ffn': out['m_g_post_ffn'], 'm_w_up': out['m_w_up'], 'm_w_fconv': out['m_w_fconv'], 'm_b_fconv': out['m_b_fconv'], 'm_w_down': out['m_w_down'], 'v_g_pre_mix': out['v_g_pre_mix'], 'v_g_post_mix': out['v_g_post_mix'], 'v_w_in': out['v_w_in'], 'v_g_ckv': out['v_g_ckv'], 'v_w_ukv': out['v_w_ukv'], 'v_w_o_mla': out['v_w_o_mla'], 'v_w_conv_lru': out['v_w_conv_lru'], 'v_b_conv_lru': out['v_b_conv_lru'], 'v_w_rg': out['v_w_rg'], 'v_b_rg': out['v_b_rg'], 'v_w_ig': out['v_w_ig'], 'v_b_ig': out['v_b_ig'], 'v_lru_lambda': out['v_lru_lambda'], 'v_w_o_lru': out['v_w_o_lru'], 'v_w_out': out['v_w_out'], 'v_g_pre_x': out['v_g_pre_x'], 'v_g_post_x': out['v_g_post_x'], 'v_g_mem': out['v_g_mem'], 'v_w_cq': out['v_w_cq'], 'v_w_ck': out['v_w_ck'], 'v_w_cv': out['v_w_cv'], 'v_w_co': out['v_w_co'], 'v_g_pre_ffn': out['v_g_pre_ffn'], 'v_g_post_ffn': out['v_g_post_ffn'], 'v_w_up': out['v_w_up'], 'v_w_fconv': out['v_w_fconv'], 'v_b_fconv': out['v_b_fconv'], 'v_w_down': out['v_w_down']}


def _loss(weights, diff, rest, loss_target):
    with _jax.named_scope("forward"):
        args = {**rest, TWIN_DIFF_INPUT: diff, **{k: w.astype(_WEIGHT_DTYPES[k]) for k, w in weights.items()}}
        y = _forward(args)
    with _jax.named_scope("loss_head"):
        err = _jnp.square(y.astype(_jnp.float32) - loss_target)
        return 0.5 * _jnp.sum(_jnp.mean(err, axis=-1)) if err.ndim else 0.5 * err


def _adamw(w, g, m, v):
    m = ADAM_B1 * m + (1.0 - ADAM_B1) * g
    v = ADAM_B2 * v + (1.0 - ADAM_B2) * _jnp.square(g)
    m_hat = m / (1.0 - ADAM_B1 ** ADAM_STEP)
    v_hat = v / (1.0 - ADAM_B2 ** ADAM_STEP)
    delta = -ADAM_LR * (m_hat / (_jnp.sqrt(v_hat) + ADAM_EPS) + ADAM_WD * w)
    return delta, m, v


def reference(x, mem, positions, g_pre_mix, g_post_mix, w_in, g_ckv, w_ukv, w_o_mla, w_conv_lru, b_conv_lru, w_rg, b_rg, w_ig, b_ig, lru_lambda, w_o_lru, w_out, g_pre_x, g_post_x, g_mem, w_cq, w_ck, w_cv, w_co, g_pre_ffn, g_post_ffn, w_up, w_fconv, b_fconv, w_down, loss_target, m_g_pre_mix, m_g_post_mix, m_w_in, m_g_ckv, m_w_ukv, m_w_o_mla, m_w_conv_lru, m_b_conv_lru, m_w_rg, m_b_rg, m_w_ig, m_b_ig, m_lru_lambda, m_w_o_lru, m_w_out, m_g_pre_x, m_g_post_x, m_g_mem, m_w_cq, m_w_ck, m_w_cv, m_w_co, m_g_pre_ffn, m_g_post_ffn, m_w_up, m_w_fconv, m_b_fconv, m_w_down, v_g_pre_mix, v_g_post_mix, v_w_in, v_g_ckv, v_w_ukv, v_w_o_mla, v_w_conv_lru, v_b_conv_lru, v_w_rg, v_b_rg, v_w_ig, v_b_ig, v_lru_lambda, v_w_o_lru, v_w_out, v_g_pre_x, v_g_post_x, v_g_mem, v_w_cq, v_w_ck, v_w_cv, v_w_co, v_g_pre_ffn, v_g_post_ffn, v_w_up, v_w_fconv, v_b_fconv, v_w_down):
    given = dict(x=x, mem=mem, positions=positions, g_pre_mix=g_pre_mix, g_post_mix=g_post_mix, w_in=w_in, g_ckv=g_ckv, w_ukv=w_ukv, w_o_mla=w_o_mla, w_conv_lru=w_conv_lru, b_conv_lru=b_conv_lru, w_rg=w_rg, b_rg=b_rg, w_ig=w_ig, b_ig=b_ig, lru_lambda=lru_lambda, w_o_lru=w_o_lru, w_out=w_out, g_pre_x=g_pre_x, g_post_x=g_post_x, g_mem=g_mem, w_cq=w_cq, w_ck=w_ck, w_cv=w_cv, w_co=w_co, g_pre_ffn=g_pre_ffn, g_post_ffn=g_post_ffn, w_up=w_up, w_fconv=w_fconv, b_fconv=b_fconv, w_down=w_down, loss_target=loss_target, m_g_pre_mix=m_g_pre_mix, m_g_post_mix=m_g_post_mix, m_w_in=m_w_in, m_g_ckv=m_g_ckv, m_w_ukv=m_w_ukv, m_w_o_mla=m_w_o_mla, m_w_conv_lru=m_w_conv_lru, m_b_conv_lru=m_b_conv_lru, m_w_rg=m_w_rg, m_b_rg=m_b_rg, m_w_ig=m_w_ig, m_b_ig=m_b_ig, m_lru_lambda=m_lru_lambda, m_w_o_lru=m_w_o_lru, m_w_out=m_w_out, m_g_pre_x=m_g_pre_x, m_g_post_x=m_g_post_x, m_g_mem=m_g_mem, m_w_cq=m_w_cq, m_w_ck=m_w_ck, m_w_cv=m_w_cv, m_w_co=m_w_co, m_g_pre_ffn=m_g_pre_ffn, m_g_post_ffn=m_g_post_ffn, m_w_up=m_w_up, m_w_fconv=m_w_fconv, m_b_fconv=m_b_fconv, m_w_down=m_w_down, v_g_pre_mix=v_g_pre_mix, v_g_post_mix=v_g_post_mix, v_w_in=v_w_in, v_g_ckv=v_g_ckv, v_w_ukv=v_w_ukv, v_w_o_mla=v_w_o_mla, v_w_conv_lru=v_w_conv_lru, v_b_conv_lru=v_b_conv_lru, v_w_rg=v_w_rg, v_b_rg=v_b_rg, v_w_ig=v_w_ig, v_b_ig=v_b_ig, v_lru_lambda=v_lru_lambda, v_w_o_lru=v_w_o_lru, v_w_out=v_w_out, v_g_pre_x=v_g_pre_x, v_g_post_x=v_g_post_x, v_g_mem=v_g_mem, v_w_cq=v_w_cq, v_w_ck=v_w_ck, v_w_cv=v_w_cv, v_w_co=v_w_co, v_g_pre_ffn=v_g_pre_ffn, v_g_post_ffn=v_g_post_ffn, v_w_up=v_w_up, v_w_fconv=v_w_fconv, v_b_fconv=v_b_fconv, v_w_down=v_w_down)
    weights = {n: given[n] for n in TWIN_WEIGHTS}
    shared = {n: given[n] for n in SHARED_INPUTS}
    per_example = {n: given[n] for n in ['x', 'mem', 'positions']}
    grad_fn = _jax.value_and_grad(_loss, argnums=(0, 1))

    def one_microbatch(ex, loss_target):
        ex = dict(ex)
        diff = ex.pop(TWIN_DIFF_INPUT)
        return grad_fn(weights, diff, {**shared, **ex}, loss_target)

    if N_MICROBATCH == 1:
        loss, (grad_w, grad_x) = one_microbatch(per_example, given["loss_target"])
    else:
        def body(carry, xs):
            loss_sum, grad_sum = carry
            l_k, (gw_k, gx_k) = one_microbatch(xs[0], xs[1])
            with _jax.named_scope("update"):
                return (loss_sum + l_k, _jax.tree.map(_jnp.add, grad_sum, gw_k)), gx_k

        init = (_jnp.zeros((), _jnp.float32), _jax.tree.map(_jnp.zeros_like, weights))
        (loss, grad_w), grad_x = _jax.lax.scan(body, init, (per_example, given["loss_target"]))
    with _jax.named_scope("update"):
        delta_w, new_m, new_v = {}, {}, {}
        for n in TWIN_WEIGHTS:
            delta_w[n], new_m[n], new_v[n] = _adamw(weights[n], grad_w[n], given["m_" + n], given["v_" + n])
    return (loss, grad_x, *[grad_w[n] for n in TWIN_WEIGHTS], *[delta_w[n] for n in TWIN_WEIGHTS],
            *[new_m[n] for n in TWIN_WEIGHTS], *[new_v[n] for n in TWIN_WEIGHTS])
```

```python
import functools
import math

import jax
import jax.numpy as jnp
from jax import lax
from jax.experimental import pallas as pl
from jax.experimental.pallas import tpu as pltpu

F32 = jnp.float32
BF16 = jnp.bfloat16
MESH = pl.DeviceIdType.MESH

EPS = 1e-6
D_MODEL = 2048
N_HEADS = 16
HEAD_W = 128
ROPE_HALF = 32
SOFTMAX_SCALE = 192 ** -0.5
ROPE_THETA = 10000.0
LRU_BLOCKS = 16
LRU_C = 8.0
X_HEADS = 4
X_SCALE = 128 ** -0.5
D_FF = 5632

ADAM_LR = 0.001
ADAM_B1 = 0.9
ADAM_B2 = 0.999
ADAM_EPS = 1e-08
ADAM_WD = 0.01
ADAM_STEP = 10

V7X_VMEM_LIMIT = 56 * 1024 * 1024
LANES = 128
SUBLANES = 8

PROJ_W = 13312
SEG_QN, SEG_QR, SEG_LX, SEG_LY, SEG_GM, SEG_GL = range(6)
CKV_BLOCK = 24
KR_BLOCK = 100

BIG = (
    ("w_in", (2048, 2960), 1), ("w_ukv", (512, 1024), 1), ("w_o_mla", (512, 2048), 0),
    ("w_conv_lru", (4, 512), 1), ("w_o_lru", (512, 2048), 0), ("w_out", (512, 2048), 0),
    ("w_cq", (512, 512), 0), ("w_ck", (512, 512), 0), ("w_cv", (512, 512), 0),
    ("w_co", (512, 512), 1), ("w_up", (2048, 2816), 1), ("w_fconv", (3, 2816), 1),
    ("w_down", (1408, 2048), 0),
)
SMALL = (
    ("g_pre_mix", (2048,)), ("g_post_mix", (2048,)), ("g_ckv", (512,)), ("b_conv_lru", (2048,)),
    ("w_rg", (16, 128, 128)), ("b_rg", (16, 128)), ("w_ig", (16, 128, 128)), ("b_ig", (16, 128)),
    ("lru_lambda", (2048,)), ("g_pre_x", (2048,)), ("g_post_x", (2048,)), ("g_mem", (2048,)),
    ("g_pre_ffn", (2048,)), ("g_post_ffn", (2048,)), ("b_fconv", (11264,)),
)
WEIGHTS = ("g_pre_mix", "g_post_mix", "w_in", "g_ckv", "w_ukv", "w_o_mla", "w_conv_lru", "b_conv_lru",
           "w_rg", "b_rg", "w_ig", "b_ig", "lru_lambda", "w_o_lru", "w_out", "g_pre_x", "g_post_x", "g_mem",
           "w_cq", "w_ck", "w_cv", "w_co", "g_pre_ffn", "g_post_ffn", "w_up", "w_fconv", "b_fconv", "w_down")
FLAT_W = 1024
PIECE_ROWS = 9728
SHARD_ROWS = 2 * PIECE_ROWS
SMALL_ROWS = 552


def _params(sem=None):
    return pltpu.CompilerParams(dimension_semantics=sem, vmem_limit_bytes=V7X_VMEM_LIMIT)


def _tile(dim, pref):
    if dim <= pref:
        return dim
    for t in range(pref, 0, -LANES):
        if dim % t == 0:
            return t
    raise ValueError((dim, pref))


def _mm(name, a, b, mode, out_dtype=F32, tm=1024, tn=1024, tk=512):
    if mode == "nn":
        (m, k), (_, n) = a.shape, b.shape
    elif mode == "nt":
        (m, k), (n, _) = a.shape, b.shape
    else:
        (k, m), (_, n) = a.shape, b.shape
    tm, tn, tk = _tile(m, tm), _tile(n, tn), _tile(k, tk)
    nk = k // tk
    if mode == "nn":
        a_spec = pl.BlockSpec((tm, tk), lambda i, j, kk: (i, kk))
        b_spec = pl.BlockSpec((tk, tn), lambda i, j, kk: (kk, j))
        dims = (((1,), (0,)), ((), ()))
    elif mode == "nt":
        a_spec = pl.BlockSpec((tm, tk), lambda i, j, kk: (i, kk))
        b_spec = pl.BlockSpec((tn, tk), lambda i, j, kk: (j, kk))
        dims = (((1,), (1,)), ((), ()))
    else:
        a_spec = pl.BlockSpec((tk, tm), lambda i, j, kk: (kk, i))
        b_spec = pl.BlockSpec((tk, tn), lambda i, j, kk: (kk, j))
        dims = (((0,), (0,)), ((), ()))

    def body(a_ref, b_ref, o_ref, acc_ref):
        kk = pl.program_id(2)

        @pl.when(kk == 0)
        def _():
            acc_ref[...] = jnp.zeros_like(acc_ref)

        acc_ref[...] += lax.dot_general(a_ref[...].astype(BF16), b_ref[...].astype(BF16), dims,
                                        preferred_element_type=F32)

        @pl.when(kk == nk - 1)
        def _():
            o_ref[...] = acc_ref[...].astype(o_ref.dtype)

    return pl.pallas_call(
        body, name=name, grid=(m // tm, n // tn, nk),
        in_specs=[a_spec, b_spec], out_specs=pl.BlockSpec((tm, tn), lambda i, j, kk: (i, j)),
        out_shape=jax.ShapeDtypeStruct((m, n), out_dtype),
        scratch_shapes=[pltpu.VMEM((tm, tn), F32)],
        compiler_params=_params(("parallel", "parallel", "arbitrary")),
    )(a, b)


def _rows(name, fn, n_rows, tm, row_ins, full_ins, row_outs, acc_outs=()):
    in_specs, args = [], []
    for arr, width, cb in row_ins:
        in_specs.append(pl.BlockSpec((tm, width), lambda i, cb=cb: (i, cb)))
        args.append(arr)
    for arr in full_ins:
        in_specs.append(pl.BlockSpec(arr.shape, lambda i, nd=arr.ndim: (0,) * nd))
        args.append(arr)
    out_shape = [jax.ShapeDtypeStruct((n_rows, w), dt) for w, dt in row_outs]
    out_specs = [pl.BlockSpec((tm, w), lambda i: (i, 0)) for w, _ in row_outs]
    for s in acc_outs:
        out_shape.append(jax.ShapeDtypeStruct(s, F32))
        out_specs.append(pl.BlockSpec(s, lambda i, nd=len(s): (0,) * nd))
    n_in, n_ro = len(args), len(row_outs)

    def body(*refs):
        res = fn(*[r[...] for r in refs[:n_in]])
        outs = refs[n_in:]
        for r, v in zip(outs[:n_ro], res[:n_ro]):
            r[...] = v.astype(r.dtype)
        first = pl.program_id(0) == 0
        for r, v in zip(outs[n_ro:], res[n_ro:]):
            @pl.when(first)
            def _(r=r):
                r[...] = jnp.zeros_like(r)

            r[...] += v

    return pl.pallas_call(
        body, name=name, grid=(n_rows // tm,), in_specs=in_specs, out_specs=out_specs, out_shape=out_shape,
        compiler_params=_params(("arbitrary",)),
    )(*args)


def _rsq(x):
    return lax.rsqrt(jnp.mean(x * x, axis=-1, keepdims=True) + EPS)


def _rms(x, g):
    return x * _rsq(x) * g


def _rms_bwd(x, g, dy):
    r = _rsq(x)
    xh = x * r
    dxh = dy * g
    dx = r * (dxh - xh * jnp.mean(dxh * xh, axis=-1, keepdims=True))
    return dx, jnp.sum(dy * xh, axis=0, keepdims=True)


def _sigmoid(x):
    return 1.0 / (1.0 + jnp.exp(-x))


_GELU_C = math.sqrt(2.0 / math.pi)


def _gelu(x):
    return 0.5 * x * (1.0 + jnp.tanh(_GELU_C * (x + 0.044715 * x * x * x)))


def _gelu_and_grad(x):
    th = jnp.tanh(_GELU_C * (x + 0.044715 * x * x * x))
    g = 0.5 * x * (1.0 + th)
    dg = 0.5 * (1.0 + th) + 0.5 * x * (1.0 - th * th) * _GELU_C * (1.0 + 3.0 * 0.044715 * x * x)
    return g, dg


def _dot(a, b):
    return jnp.dot(a.astype(BF16), b.astype(BF16), preferred_element_type=F32)


def _dot_nt(a, b):
    return lax.dot_general(a.astype(BF16), b.astype(BF16), (((1,), (1,)), ((), ())), preferred_element_type=F32)


def _dot_tn(a, b):
    return lax.dot_general(a.astype(BF16), b.astype(BF16), (((0,), (0,)), ((), ())), preferred_element_type=F32)


def _row_iota(shape):
    return lax.broadcasted_iota(jnp.int32, shape, 0)


def _shift_down(x, k, halo):
    xs = pltpu.roll(x, k, 0)
    hs = pltpu.roll(halo, k, 0)
    first = jnp.where(_row_iota(hs.shape) < k, hs, xs[:SUBLANES])
    return jnp.concatenate([first, xs[SUBLANES:]], axis=0)


def _shift_up(x, k, halo):
    tm = x.shape[0]
    xs = pltpu.roll(x, tm - k, 0)
    hs = pltpu.roll(halo, SUBLANES - k, 0)
    last = jnp.where(_row_iota(hs.shape) >= SUBLANES - k, hs, xs[tm - SUBLANES:])
    return jnp.concatenate([xs[:tm - SUBLANES], last], axis=0)


def _conv_taps(x, halo, n_taps):
    return [x] + [_shift_down(x, k, halo) for k in range(1, n_taps)]


def _causal_conv(taps, w, b):
    kw = len(taps)
    y = b + w[kw - 1:kw] * taps[0]
    for s in range(1, kw):
        y = y + w[kw - 1 - s:kw - s] * taps[s]
    return y


def _rope_group(g, c, s1, s2):
    return g * c + pltpu.roll(g, LANES - ROPE_HALF, 1) * s1 + pltpu.roll(g, ROPE_HALF, 1) * s2


def _rope_heads(q, c, s1, s2):
    return jnp.concatenate(
        [_rope_group(q[:, h * HEAD_W:(h + 1) * HEAD_W], c, s1, s2) for h in range(N_HEADS)], axis=1)


def _rope_tables(pos_col, tm):
    inv_freq = ROPE_THETA ** (-jnp.arange(0, 2 * ROPE_HALF, 2, dtype=F32) / (2 * ROPE_HALF))
    invf = jnp.concatenate([inv_freq, inv_freq, jnp.zeros((LANES - 2 * ROPE_HALF,), F32)])[None, :]

    def fn(pos, invf):
        ang = pos.astype(F32) * invf
        c, s = jnp.cos(ang), jnp.sin(ang)
        lane = lax.broadcasted_iota(jnp.int32, ang.shape, 1)
        s1 = jnp.where(lane < ROPE_HALF, -s, 0.0)
        s2 = jnp.where((lane >= ROPE_HALF) & (lane < 2 * ROPE_HALF), s, 0.0)
        return c, s1, s2

    n = pos_col.shape[0]
    return _rows("rope_tables", fn, n, tm, [(pos_col, 1, 0)], [invf], [(LANES, F32)] * 3)


def _attn_tiles(t):
    tq = min(512, t)
    return tq, t // tq


def _scores(qn_ref, qr_ref, kv_ref, kr_ref, qi, ki, tq):
    q = jnp.concatenate([qn_ref[...].astype(BF16), qr_ref[...]], axis=1)
    k = jnp.concatenate([kv_ref[:, :HEAD_W], kr_ref[...]], axis=1)
    s = _dot_nt(q, k) * SOFTMAX_SCALE
    row = qi * tq + lax.broadcasted_iota(jnp.int32, s.shape, 0)
    col = ki * tq + lax.broadcasted_iota(jnp.int32, s.shape, 1)
    return jnp.where(col <= row, s, -1e30), q, k


def _flash_fwd(proj, qr, kv, kr):
    t = proj.shape[0]
    tq, nq = _attn_tiles(t)

    def body(qn_ref, qr_ref, kv_ref, kr_ref, o_ref, lse_ref, m_s, l_s, acc_s):
        qi, ki = pl.program_id(1), pl.program_id(2)

        @pl.when(ki == 0)
        def _():
            m_s[...] = jnp.full_like(m_s, -1e30)
            l_s[...] = jnp.zeros_like(l_s)
            acc_s[...] = jnp.zeros_like(acc_s)

        @pl.when(ki <= qi)
        def _():
            s, _, _ = _scores(qn_ref, qr_ref, kv_ref, kr_ref, qi, ki, tq)
            m_prev = m_s[...]
            m_new = jnp.maximum(m_prev, jnp.max(s, axis=1, keepdims=True))
            alpha = jnp.exp(m_prev - m_new)
            p = jnp.exp(s - m_new)
            l_s[...] = alpha * l_s[...] + jnp.sum(p, axis=1, keepdims=True)
            acc_s[...] = alpha * acc_s[...] + _dot(p, kv_ref[:, HEAD_W:])
            m_s[...] = m_new

        @pl.when(ki == nq - 1)
        def _():
            o_ref[...] = acc_s[...] / l_s[...]
            lse_ref[...] = m_s[...] + jnp.log(l_s[...])

    kblk = lambda h, qi, ki: jnp.minimum(ki, qi)
    return pl.pallas_call(
        body, name="mla_fwd", grid=(N_HEADS, nq, nq),
        in_specs=[pl.BlockSpec((tq, HEAD_W), lambda h, qi, ki: (qi, h)),
                  pl.BlockSpec((tq, HEAD_W), lambda h, qi, ki: (qi, h)),
                  pl.BlockSpec((tq, 2 * HEAD_W), lambda h, qi, ki: (kblk(h, qi, ki), h)),
                  pl.BlockSpec((tq, HEAD_W), lambda h, qi, ki: (kblk(h, qi, ki), 0))],
        out_specs=[pl.BlockSpec((tq, HEAD_W), lambda h, qi, ki: (qi, h)),
                   pl.BlockSpec((None, tq, 1), lambda h, qi, ki: (h, qi, 0))],
        out_shape=[jax.ShapeDtypeStruct((t, N_HEADS * HEAD_W), F32),
                   jax.ShapeDtypeStruct((N_HEADS, t, 1), F32)],
        scratch_shapes=[pltpu.VMEM((tq, 1), F32), pltpu.VMEM((tq, 1), F32), pltpu.VMEM((tq, HEAD_W), F32)],
        compiler_params=_params(("parallel", "parallel", "arbitrary")),
    )(proj, qr, kv, kr)


def _flash_dq(proj, qr, kv, kr, do, o, lse):
    t = proj.shape[0]
    tq, nq = _attn_tiles(t)

    def body(qn_ref, qr_ref, kv_ref, kr_ref, do_ref, o_ref, lse_ref, dqn_ref, dqr_ref, dq_s, dsum_s):
        qi, ki = pl.program_id(1), pl.program_id(2)

        @pl.when(ki == 0)
        def _():
            dq_s[...] = jnp.zeros_like(dq_s)
            dsum_s[...] = jnp.sum(do_ref[...] * o_ref[...], axis=1, keepdims=True)

        @pl.when(ki <= qi)
        def _():
            s, _, k = _scores(qn_ref, qr_ref, kv_ref, kr_ref, qi, ki, tq)
            p = jnp.exp(s - lse_ref[...])
            dp = _dot_nt(do_ref[...], kv_ref[:, HEAD_W:])
            ds = p * (dp - dsum_s[...]) * SOFTMAX_SCALE
            dq_s[...] += _dot(ds, k)

        @pl.when(ki == nq - 1)
        def _():
            dqn_ref[...] = dq_s[:, :HEAD_W].astype(dqn_ref.dtype)
            dqr_ref[...] = dq_s[:, HEAD_W:]

    kblk = lambda h, qi, ki: jnp.minimum(ki, qi)
    qspec = pl.BlockSpec((tq, HEAD_W), lambda h, qi, ki: (qi, h))
    return pl.pallas_call(
        body, name="mla_dq", grid=(N_HEADS, nq, nq),
        in_specs=[qspec, qspec,
                  pl.BlockSpec((tq, 2 * HEAD_W), lambda h, qi, ki: (kblk(h, qi, ki), h)),
                  pl.BlockSpec((tq, HEAD_W), lambda h, qi, ki: (kblk(h, qi, ki), 0)),
                  qspec, qspec,
                  pl.BlockSpec((None, tq, 1), lambda h, qi, ki: (h, qi, 0))],
        out_specs=[qspec, qspec],
        out_shape=[jax.ShapeDtypeStruct((t, N_HEADS * HEAD_W), BF16),
                   jax.ShapeDtypeStruct((t, N_HEADS * HEAD_W), F32)],
        scratch_shapes=[pltpu.VMEM((tq, 2 * HEAD_W), F32), pltpu.VMEM((tq, 1), F32)],
        compiler_params=_params(("parallel", "parallel", "arbitrary")),
    )(proj, qr, kv, kr, do, o, lse)


def _flash_dkv(proj, qr, kv, kr, do, o, lse):
    t = proj.shape[0]
    tq, nq = _attn_tiles(t)

    def body(qn_ref, qr_ref, kv_ref, kr_ref, do_ref, o_ref, lse_ref, dkv_ref, dkr_ref, dk_s, dv_s):
        ki, qi = pl.program_id(1), pl.program_id(2)

        @pl.when(qi == 0)
        def _():
            dk_s[...] = jnp.zeros_like(dk_s)
            dv_s[...] = jnp.zeros_like(dv_s)

        @pl.when(qi >= ki)
        def _():
            s, q, _ = _scores(qn_ref, qr_ref, kv_ref, kr_ref, qi, ki, tq)
            p = jnp.exp(s - lse_ref[...])
            do = do_ref[...]
            dsum = jnp.sum(do * o_ref[...], axis=1, keepdims=True)
            dv_s[...] += _dot_tn(p, do)
            dp = _dot_nt(do, kv_ref[:, HEAD_W:])
            ds = p * (dp - dsum) * SOFTMAX_SCALE
            dk_s[...] += _dot_tn(ds, q)

        @pl.when(qi == nq - 1)
        def _():
            dkv_ref[...] = jnp.concatenate([dk_s[:, :HEAD_W], dv_s[...]], axis=1).astype(dkv_ref.dtype)
            dkr_ref[...] = dk_s[:, HEAD_W:]

    qblk = lambda h, ki, qi: jnp.maximum(qi, ki)
    qspec = pl.BlockSpec((tq, HEAD_W), lambda h, ki, qi: (qblk(h, ki, qi), h))
    return pl.pallas_call(
        body, name="mla_dkv", grid=(N_HEADS, nq, nq),
        in_specs=[qspec, qspec,
                  pl.BlockSpec((tq, 2 * HEAD_W), lambda h, ki, qi: (ki, h)),
                  pl.BlockSpec((tq, HEAD_W), lambda h, ki, qi: (ki, 0)),
                  qspec, qspec,
                  pl.BlockSpec((None, tq, 1), lambda h, ki, qi: (h, qblk(h, ki, qi), 0))],
        out_specs=[pl.BlockSpec((tq, 2 * HEAD_W), lambda h, ki, qi: (ki, h)),
                   pl.BlockSpec((None, tq, HEAD_W), lambda h, ki, qi: (h, ki, 0))],
        out_shape=[jax.ShapeDtypeStruct((t, N_HEADS * 2 * HEAD_W), BF16),
                   jax.ShapeDtypeStruct((N_HEADS, t, HEAD_W), F32)],
        scratch_shapes=[pltpu.VMEM((tq, 2 * HEAD_W), F32), pltpu.VMEM((tq, HEAD_W), F32)],
        compiler_params=_params(("parallel", "parallel", "arbitrary")),
    )(proj, qr, kv, kr, do, o, lse)


def _block_diag(x, w):
    return jnp.concatenate(
        [_dot(x[:, b * LANES:(b + 1) * LANES], w[b]) for b in range(LRU_BLOCKS)], axis=1)


def _block_diag_t(d, w):
    return jnp.concatenate(
        [_dot_nt(d[:, b * LANES:(b + 1) * LANES], w[b]) for b in range(LRU_BLOCKS)], axis=1)


def _lru_gates(xc, w_rg, b_rg, w_ig, b_ig, lam):
    r = _sigmoid(_block_diag(xc, w_rg) + b_rg)
    ig = _sigmoid(_block_diag(xc, w_ig) + b_ig)
    sp = jnp.maximum(-lam, 0.0) + jnp.log1p(jnp.exp(-jnp.abs(lam)))
    log_a = -LRU_C * r * sp
    a = jnp.exp(log_a)
    mult = jnp.sqrt(-jnp.tanh(log_a) * (a * a + 1.0))
    return r, ig, sp, a, mult


def _lru_tm(t):
    return min(128, t)


def _halo_spec(tm, width, cb):
    return pl.BlockSpec((SUBLANES, width), lambda i, cb=cb: (jnp.maximum(i * (tm // SUBLANES) - 1, 0), cb))


def _lru_fwd(proj, w_conv, b_conv, w_rg, b_rg, w_ig, b_ig, lam):
    t = proj.shape[0]
    tm = _lru_tm(t)
    w = D_MODEL
    n_scan = int(math.log2(tm))

    def body(lx_ref, halo_ref, ly_ref, wc_ref, bc_ref, wrg_ref, brg_ref, wig_ref, big_ref, lam_ref,
             oli_ref, h_ref, carry_s):
        i = pl.program_id(0)

        @pl.when(i == 0)
        def _():
            carry_s[...] = jnp.zeros_like(carry_s)

        x = lx_ref[...]
        halo = jnp.where(i > 0, halo_ref[...], 0.0)
        xc = _causal_conv(_conv_taps(x, halo, 4), wc_ref[...], bc_ref[...])
        _, ig, _, a, mult = _lru_gates(xc, wrg_ref[...], brg_ref[...], wig_ref[...], big_ref[...], lam_ref[...])
        u = mult * (ig * xc)
        rows = _row_iota(a.shape)
        for s in range(n_scan):
            d = 1 << s
            keep = rows >= d
            a_s = jnp.where(keep, pltpu.roll(a, d, 0), 1.0)
            u_s = jnp.where(keep, pltpu.roll(u, d, 0), 0.0)
            u = a * u_s + u
            a = a * a_s
        h = u + a * carry_s[SUBLANES - 1:SUBLANES, :]
        h_ref[...] = h
        carry_s[...] = h[tm - SUBLANES:, :]
        oli_ref[...] = (h * _gelu(ly_ref[...])).astype(oli_ref.dtype)

    full = lambda arr: pl.BlockSpec(arr.shape, lambda i, nd=arr.ndim: (0,) * nd)
    return pl.pallas_call(
        body, name="lru_fwd", grid=(t // tm,),
        in_specs=[pl.BlockSpec((tm, w), lambda i: (i, SEG_LX)), _halo_spec(tm, w, SEG_LX),
                  pl.BlockSpec((tm, w), lambda i: (i, SEG_LY)),
                  full(w_conv), full(b_conv), full(w_rg), full(b_rg), full(w_ig), full(b_ig), full(lam)],
        out_specs=[pl.BlockSpec((tm, w), lambda i: (i, 0)), pl.BlockSpec((tm, w), lambda i: (i, 0))],
        out_shape=[jax.ShapeDtypeStruct((t, w), BF16), jax.ShapeDtypeStruct((t, w), F32)],
        scratch_shapes=[pltpu.VMEM((SUBLANES, w), F32)],
        compiler_params=_params(("arbitrary",)),
    )(proj, proj, proj, w_conv, b_conv, w_rg, b_rg, w_ig, b_ig, lam)


def _lru_bwd(proj, hl, doli, w_conv, b_conv, w_rg, b_rg, w_ig, b_ig, lam):
    t = proj.shape[0]
    tm = _lru_tm(t)
    nt = t // tm
    w = D_MODEL
    n_scan = int(math.log2(tm))

    def body(lx_ref, lxh_ref, ly_ref, h_ref, hh_ref, doli_ref,
             wc_ref, bc_ref, wrg_ref, brg_ref, wig_ref, big_ref, lam_ref,
             dlx_ref, dly_ref, dwc_ref, dbc_ref, dwrg_ref, dbrg_ref, dwig_ref, dbig_ref, dlam_ref,
             ca_s, cd_s, cx_s):
        i = pl.program_id(0)
        blk = nt - 1 - i

        @pl.when(i == 0)
        def _():
            for r in (ca_s, cd_s, cx_s, dwc_ref, dbc_ref, dwrg_ref, dbrg_ref, dwig_ref, dbig_ref, dlam_ref):
                r[...] = jnp.zeros_like(r)

        x = lx_ref[...]
        halo = jnp.where(blk > 0, lxh_ref[...], 0.0)
        taps = _conv_taps(x, halo, 4)
        wc = wc_ref[...]
        xc = _causal_conv(taps, wc, bc_ref[...])
        w_rg, w_ig, lam_v = wrg_ref[...], wig_ref[...], lam_ref[...]
        r, ig, sp, a, mult = _lru_gates(xc, w_rg, brg_ref[...], w_ig, big_ref[...], lam_v)
        h = h_ref[...]
        h_prev = _shift_down(h, 1, jnp.where(blk > 0, hh_ref[...], 0.0))
        gl, dgl = _gelu_and_grad(ly_ref[...])
        doli = doli_ref[...]
        dly_ref[...] = (doli * h * dgl).astype(dly_ref.dtype)
        acc_b = doli * gl
        acc_a = _shift_up(a, 1, ca_s[...])
        rows = _row_iota(a.shape)
        for s in range(n_scan):
            d = 1 << s
            keep = rows < tm - d
            a_s = jnp.where(keep, pltpu.roll(acc_a, tm - d, 0), 1.0)
            b_s = jnp.where(keep, pltpu.roll(acc_b, tm - d, 0), 0.0)
            acc_b = acc_b + acc_a * b_s
            acc_a = acc_a * a_s
        dht = acc_b + acc_a * cd_s[0:1, :]
        da = dht * h_prev
        dmult = dht * (ig * xc)
        di = dht * mult * xc
        dxc = dht * mult * ig
        dlog_a = da * a - dmult * (a * a) / mult
        dr = dlog_a * (-LRU_C * sp)
        dlam_ref[...] += jnp.sum(dlog_a * r, axis=0, keepdims=True) * (LRU_C * _sigmoid(-lam_v))
        dpr = dr * r * (1.0 - r)
        dpi = di * ig * (1.0 - ig)
        dxc = dxc + _block_diag_t(dpr, w_rg) + _block_diag_t(dpi, w_ig)
        dbrg_ref[...] += jnp.sum(dpr, axis=0, keepdims=True)
        dbig_ref[...] += jnp.sum(dpi, axis=0, keepdims=True)
        for b in range(LRU_BLOCKS):
            sl = slice(b * LANES, (b + 1) * LANES)
            dwrg_ref[b] += _dot_tn(xc[:, sl], dpr[:, sl])
            dwig_ref[b] += _dot_tn(xc[:, sl], dpi[:, sl])
        dbc_ref[...] += jnp.sum(dxc, axis=0, keepdims=True)
        dwc_ref[...] += jnp.concatenate(
            [jnp.sum(dxc * taps[3 - k], axis=0, keepdims=True) for k in range(4)], axis=0)
        cx = cx_s[...]
        dlx = wc[3:4] * dxc
        for s in range(1, 4):
            dlx = dlx + wc[3 - s:4 - s] * _shift_up(dxc, s, cx)
        dlx_ref[...] = dlx.astype(dlx_ref.dtype)
        ca_s[...] = a[:SUBLANES, :]
        cd_s[...] = dht[:SUBLANES, :]
        cx_s[...] = dxc[:SUBLANES, :]

    full = lambda arr: pl.BlockSpec(arr.shape, lambda i, nd=arr.ndim: (0,) * nd)
    rev = lambda cb: pl.BlockSpec((tm, w), lambda i, cb=cb: (nt - 1 - i, cb))
    halo = lambda cb: pl.BlockSpec(
        (SUBLANES, w), lambda i, cb=cb: (jnp.maximum((nt - 1 - i) * (tm // SUBLANES) - 1, 0), cb))
    acc = lambda s: pl.BlockSpec(s, lambda i, nd=len(s): (0,) * nd)
    acc_shapes = [(4, w), (1, w), (LRU_BLOCKS, LANES, LANES), (1, w), (LRU_BLOCKS, LANES, LANES), (1, w), (1, w)]
    return pl.pallas_call(
        body, name="lru_bwd", grid=(nt,),
        in_specs=[rev(SEG_LX), halo(SEG_LX), rev(SEG_LY), rev(0), halo(0), rev(0),
                  full(w_conv), full(b_conv), full(w_rg), full(b_rg), full(w_ig), full(b_ig), full(lam)],
        out_specs=[rev(0), rev(0)] + [acc(s) for s in acc_shapes],
        out_shape=[jax.ShapeDtypeStruct((t, w), BF16), jax.ShapeDtypeStruct((t, w), BF16)]
        + [jax.ShapeDtypeStruct(s, F32) for s in acc_shapes],
        scratch_shapes=[pltpu.VMEM((SUBLANES, w), F32)] * 3,
        compiler_params=_params(("arbitrary",)),
    )(proj, proj, proj, hl, hl, doli, w_conv, b_conv, w_rg, b_rg, w_ig, b_ig, lam)


FFN_TC = 512
FFN_NJ = D_FF // FFN_TC


def _ffn_tm(t):
    return min(512, t)


def _ffn_fwd(up_pre, w_fconv, b_fconv):
    t = up_pre.shape[0]
    tm = _ffn_tm(t)

    def body(g_ref, gh_ref, v_ref, vh_ref, wg_ref, wv_ref, bg_ref, bv_ref, act_ref):
        first = pl.program_id(0) > 0
        gate = _causal_conv(_conv_taps(g_ref[...], jnp.where(first, gh_ref[...], 0.0), 3), wg_ref[...], bg_ref[...])
        val = _causal_conv(_conv_taps(v_ref[...], jnp.where(first, vh_ref[...], 0.0), 3), wv_ref[...], bv_ref[...])
        act_ref[...] = (_gelu(gate) * val).astype(act_ref.dtype)

    blk = lambda off: pl.BlockSpec((tm, FFN_TC), lambda i, j: (i, j + off))
    halo = lambda off: pl.BlockSpec(
        (SUBLANES, FFN_TC), lambda i, j: (jnp.maximum(i * (tm // SUBLANES) - 1, 0), j + off))
    wsp = lambda rows, off: pl.BlockSpec((rows, FFN_TC), lambda i, j: (0, j + off))
    return pl.pallas_call(
        body, name="ffn_conv_fwd", grid=(t // tm, FFN_NJ),
        in_specs=[blk(0), halo(0), blk(FFN_NJ), halo(FFN_NJ), wsp(3, 0), wsp(3, FFN_NJ), wsp(1, 0), wsp(1, FFN_NJ)],
        out_specs=pl.BlockSpec((tm, FFN_TC), lambda i, j: (i, j)),
        out_shape=jax.ShapeDtypeStruct((t, D_FF), BF16),
        compiler_params=_params(("parallel", "parallel")),
    )(up_pre, up_pre, up_pre, up_pre, w_fconv, w_fconv, b_fconv, b_fconv)


def _ffn_bwd_act(up_pre, dact, w_fconv, b_fconv):
    t = up_pre.shape[0]
    tm = _ffn_tm(t)

    def body(g_ref, gh_ref, v_ref, vh_ref, da_ref, wg_ref, wv_ref, bg_ref, bv_ref, dup_ref, dw_ref, db_ref):
        jj, i = pl.program_id(0), pl.program_id(1)

        @pl.when(i == 0)
        def _():
            dw_ref[...] = jnp.zeros_like(dw_ref)
            db_ref[...] = jnp.zeros_like(db_ref)

        g_taps = _conv_taps(g_ref[...], jnp.where(i > 0, gh_ref[...], 0.0), 3)
        v_taps = _conv_taps(v_ref[...], jnp.where(i > 0, vh_ref[...], 0.0), 3)
        gate = _causal_conv(g_taps, wg_ref[...], bg_ref[...])
        val = _causal_conv(v_taps, wv_ref[...], bv_ref[...])
        gl, dgl = _gelu_and_grad(gate)
        dact = da_ref[...]
        is_gate = jj < FFN_NJ
        d = jnp.where(is_gate, dact * val * dgl, dact * gl)
        dup_ref[...] = d
        db_ref[...] += jnp.sum(d, axis=0, keepdims=True)
        dw_ref[...] += jnp.concatenate(
            [jnp.sum(d * jnp.where(is_gate, g_taps[2 - k], v_taps[2 - k]), axis=0, keepdims=True)
             for k in range(3)], axis=0)

    jm = lambda jj: jj % FFN_NJ
    blk = lambda off: pl.BlockSpec((tm, FFN_TC), lambda jj, i: (i, jm(jj) + off))
    halo = lambda off: pl.BlockSpec(
        (SUBLANES, FFN_TC), lambda jj, i: (jnp.maximum(i * (tm // SUBLANES) - 1, 0), jm(jj) + off))
    wsp = lambda rows, off: pl.BlockSpec((rows, FFN_TC), lambda jj, i: (0, jm(jj) + off))
    return pl.pallas_call(
        body, name="ffn_bwd_act", grid=(2 * FFN_NJ, t // tm),
        in_specs=[blk(0), halo(0), blk(FFN_NJ), halo(FFN_NJ), blk(0),
                  wsp(3, 0), wsp(3, FFN_NJ), wsp(1, 0), wsp(1, FFN_NJ)],
        out_specs=[pl.BlockSpec((tm, FFN_TC), lambda jj, i: (i, jj)),
                   pl.BlockSpec((3, FFN_TC), lambda jj, i: (0, jj)),
                   pl.BlockSpec((1, FFN_TC), lambda jj, i: (0, jj))],
        out_shape=[jax.ShapeDtypeStruct((t, 2 * D_FF), F32), jax.ShapeDtypeStruct((3, 2 * D_FF), F32),
                   jax.ShapeDtypeStruct((1, 2 * D_FF), F32)],
        compiler_params=_params(("parallel", "arbitrary")),
    )(up_pre, up_pre, up_pre, up_pre, dact, w_fconv, w_fconv, b_fconv, b_fconv)


def _ffn_bwd_conv(dup, w_fconv):
    t = dup.shape[0]
    tm = _ffn_tm(t)
    nt = t // tm

    def body(d_ref, dh_ref, w_ref, out_ref):
        d = d_ref[...]
        halo = jnp.where(pl.program_id(0) < nt - 1, dh_ref[...], 0.0)
        wv = w_ref[...]
        out = wv[2:3] * d + wv[1:2] * _shift_up(d, 1, halo) + wv[0:1] * _shift_up(d, 2, halo)
        out_ref[...] = out.astype(out_ref.dtype)

    return pl.pallas_call(
        body, name="ffn_bwd_conv", grid=(nt, 2 * FFN_NJ),
        in_specs=[pl.BlockSpec((tm, FFN_TC), lambda i, j: (i, j)),
                  pl.BlockSpec((SUBLANES, FFN_TC),
                               lambda i, j: (jnp.minimum((i + 1) * (tm // SUBLANES), t // SUBLANES - 1), j)),
                  pl.BlockSpec((3, FFN_TC), lambda i, j: (0, j))],
        out_specs=pl.BlockSpec((tm, FFN_TC), lambda i, j: (i, j)),
        out_shape=jax.ShapeDtypeStruct((t, 2 * D_FF), BF16),
        compiler_params=_params(("parallel", "parallel")),
    )(dup, dup, w_fconv)


def _xattn_probs(cq, ck, h):
    sl = slice(h * LANES, (h + 1) * LANES)
    s = _dot_nt(cq[:, sl], ck[:, sl]) * X_SCALE
    e = jnp.exp(s - jnp.max(s, axis=1, keepdims=True))
    return e / jnp.sum(e, axis=1, keepdims=True), sl


def _xattn_fwd_fn(cq, ck, cv):
    outs = []
    for h in range(X_HEADS):
        p, sl = _xattn_probs(cq, ck, h)
        outs.append(_dot(p, cv[:, sl]))
    return (jnp.concatenate(outs, axis=1),)


def _xattn_bwd_fn(cq, dco, ck, cv):
    dcq, dck, dcv = [], [], []
    for h in range(X_HEADS):
        p, sl = _xattn_probs(cq, ck, h)
        dcv.append(_dot_tn(p, dco[:, sl]))
        dp = _dot_nt(dco[:, sl], cv[:, sl])
        ds = p * (dp - jnp.sum(p * dp, axis=1, keepdims=True)) * X_SCALE
        dcq.append(_dot(ds, ck[:, sl]))
        dck.append(_dot_tn(ds, cq[:, sl]))
    return jnp.concatenate(dcq, axis=1), jnp.concatenate(dck, axis=1), jnp.concatenate(dcv, axis=1)


def _perm_w_in(w):
    q = w[:, :3072].reshape(D_MODEL, N_HEADS, 192)
    qn = q[:, :, :128].reshape(D_MODEL, 2048)
    qr = jnp.pad(q[:, :, 128:], ((0, 0), (0, 0), (0, 64))).reshape(D_MODEL, 2048)
    ckv = w[:, 3072:3584]
    kr = jnp.pad(w[:, 3584:3648], ((0, 0), (0, 64)))
    rest = w[:, 3648:]
    pad = jnp.zeros((D_MODEL, PROJ_W - 12928), w.dtype)
    return jnp.concatenate([qn, qr, rest, ckv, kr, pad], axis=1)


def _unperm_dw_in(dw):
    qn = dw[:, :2048].reshape(D_MODEL, N_HEADS, 128)
    qr = dw[:, 2048:4096].reshape(D_MODEL, N_HEADS, 128)[:, :, :64]
    q = jnp.concatenate([qn, qr], axis=2).reshape(D_MODEL, 3072)
    return jnp.concatenate([q, dw[:, 12288:12800], dw[:, 12800:12864], dw[:, 4096:12288]], axis=1)


def _local_step(x, mem, positions, tgt, wb, ws):
    t = x.shape[0]
    n_mem = mem.shape[0]
    tm = min(256, t)
    row = lambda v: v.reshape(1, -1)
    g_pre_mix, g_post_mix, g_ckv = row(ws["g_pre_mix"]), row(ws["g_post_mix"]), row(ws["g_ckv"])
    g_pre_x, g_post_x, g_mem = row(ws["g_pre_x"]), row(ws["g_post_x"]), row(ws["g_mem"])
    g_pre_ffn, g_post_ffn = row(ws["g_pre_ffn"]), row(ws["g_post_ffn"])
    b_conv, lam, b_fconv = row(ws["b_conv_lru"]), row(ws["lru_lambda"]), row(ws["b_fconv"])
    b_rg, b_ig = row(ws["b_rg"]), row(ws["b_ig"])
    w_rg, w_ig = ws["w_rg"].astype(BF16), ws["w_ig"].astype(BF16)
    w_in = _perm_w_in(wb["w_in"])
    w_conv = wb["w_conv_lru"].astype(F32)
    w_fconv = wb["w_fconv"].astype(F32)
    W = D_MODEL

    cos_t, s1_t, s2_t = _rope_tables(positions.reshape(t, 1), tm)
    (h1,) = _rows("rms_pre_mix", lambda x, g: (_rms(x, g),), t, tm, [(x, W, 0)], [g_pre_mix], [(W, BF16)])
    proj = _mm("proj", h1, w_in, "nn")

    def rope_ckv_fn(qr, ckv, kr, c, s1, s2, g):
        return _rope_heads(qr, c, s1, s2), _rope_group(kr, c, s1, s2), _rms(ckv, g)

    qr, kr, ckvn = _rows("rope_ckv", rope_ckv_fn, t, tm,
                         [(proj, W, SEG_QR), (proj, 512, CKV_BLOCK), (proj, LANES, KR_BLOCK),
                          (cos_t, LANES, 0), (s1_t, LANES, 0), (s2_t, LANES, 0)], [g_ckv],
                         [(W, BF16), (LANES, BF16), (512, BF16)])
    kv = _mm("kv_up", ckvn, wb["w_ukv"], "nn", out_dtype=BF16)
    o, lse = _flash_fwd(proj, qr, kv, kr)
    o_mla = _mm("o_mla", o, wb["w_o_mla"], "nn")
    oli, hl = _lru_fwd(proj, w_conv, b_conv, w_rg, b_rg, w_ig, b_ig, lam)
    o_lru = _mm("o_lru", oli, wb["w_o_lru"], "nn")
    (merged,) = _rows("merge", lambda gm, gl, a, b: (_sigmoid(gm) * a + _sigmoid(gl) * b,), t, tm,
                      [(proj, W, SEG_GM), (proj, W, SEG_GL), (o_mla, W, 0), (o_lru, W, 0)], [], [(W, BF16)])
    z1 = _mm("w_out", merged, wb["w_out"], "nn")

    def post_fn(x, z, g_post, g_pre):
        x1 = x + _rms(z, g_post)
        return x1, _rms(x1, g_pre)

    x1, h2 = _rows("post_mix", post_fn, t, tm, [(x, W, 0), (z1, W, 0)], [g_post_mix, g_pre_x],
                   [(W, F32), (W, BF16)])
    cq = _mm("cq", h2, wb["w_cq"], "nn", out_dtype=BF16)
    (mn,) = _rows("rms_mem", lambda m, g: (_rms(m, g),), n_mem, n_mem, [(mem, W, 0)], [g_mem], [(W, BF16)])
    ck = _mm("ck", mn, wb["w_ck"], "nn", out_dtype=BF16)
    cv = _mm("cv", mn, wb["w_cv"], "nn", out_dtype=BF16)
    (co,) = _rows("xattn_fwd", _xattn_fwd_fn, t, tm, [(cq, 512, 0)], [ck, cv], [(512, BF16)])
    z2 = _mm("w_co", co, wb["w_co"], "nn")
    x2, h3 = _rows("post_x", post_fn, t, tm, [(x1, W, 0), (z2, W, 0)], [g_post_x, g_pre_ffn],
                   [(W, F32), (W, BF16)])
    up_pre = _mm("w_up", h3, wb["w_up"], "nn")
    act = _ffn_fwd(up_pre, w_fconv, b_fconv)
    z3 = _mm("w_down", act, wb["w_down"], "nn")

    def loss_fn(x2, z3, tgt, g):
        err = x2 + _rms(z3, g) - tgt
        dy = err * (1.0 / W)
        dz, dg = _rms_bwd(z3, g, dy)
        part = 0.5 * jnp.sum(err * err) * (1.0 / W)
        return dy, dz, jnp.zeros((SUBLANES, LANES), F32) + part, dg

    dy, dz3, loss_acc, dg_post_ffn = _rows(
        "loss", loss_fn, t, tm, [(x2, W, 0), (z3, W, 0), (tgt, W, 0)], [g_post_ffn],
        [(W, F32), (W, BF16)], [(SUBLANES, LANES), (1, W)])
    grads = {"g_post_ffn": dg_post_ffn}
    dact = _mm("d_act", dz3, wb["w_down"], "nt")
    grads["w_down"] = _mm("dw_down", act, dz3, "tn")
    dup, grads["w_fconv"], grads["b_fconv"] = _ffn_bwd_act(up_pre, dact, w_fconv, b_fconv)
    dup_pre = _ffn_bwd_conv(dup, w_fconv)
    dh3 = _mm("d_h3", dup_pre, wb["w_up"], "nt")
    grads["w_up"] = _mm("dw_up", h3, dup_pre, "tn")

    def res_bwd_fn(dres, xa, dh, z, g_pre, g_post):
        dxa, dg_pre = _rms_bwd(xa, g_pre, dh)
        dxa = dres + dxa
        dz, dg_post = _rms_bwd(z, g_post, dxa)
        return dxa, dz, dg_pre, dg_post

    dx2, dz2, grads["g_pre_ffn"], grads["g_post_x"] = _rows(
        "bwd_post_x", res_bwd_fn, t, tm, [(dy, W, 0), (x2, W, 0), (dh3, W, 0), (z2, W, 0)],
        [g_pre_ffn, g_post_x], [(W, F32), (W, BF16)], [(1, W), (1, W)])
    dco = _mm("d_co", dz2, wb["w_co"], "nt")
    grads["w_co"] = _mm("dw_co", co, dz2, "tn")
    dcq, dck, dcv = _rows("xattn_bwd", _xattn_bwd_fn, t, tm, [(cq, 512, 0), (dco, 512, 0)], [ck, cv],
                          [(512, BF16)], [(n_mem, 512), (n_mem, 512)])
    dh2 = _mm("d_h2", dcq, wb["w_cq"], "nt")
    grads["w_cq"] = _mm("dw_cq", h2, dcq, "tn")
    grads["w_ck"] = _mm("dw_ck", mn, dck, "tn")
    grads["w_cv"] = _mm("dw_cv", mn, dcv, "tn")
    dmn_k = _mm("d_mn_k", dck, wb["w_ck"], "nt")
    dmn_v = _mm("d_mn_v", dcv, wb["w_cv"], "nt")
    (grads["g_mem"],) = _rows("dg_mem", lambda m, a, b: (jnp.sum((a + b) * m * _rsq(m), axis=0, keepdims=True),),
                              n_mem, n_mem, [(mem, W, 0), (dmn_k, W, 0), (dmn_v, W, 0)], [], [], [(1, W)])
    dx1, dz1, grads["g_pre_x"], grads["g_post_mix"] = _rows(
        "bwd_post_mix", res_bwd_fn, t, tm, [(dx2, W, 0), (x1, W, 0), (dh2, W, 0), (z1, W, 0)],
        [g_pre_x, g_post_mix], [(W, F32), (W, BF16)], [(1, W), (1, W)])
    dmerged = _mm("d_merged", dz1, wb["w_out"], "nt")
    grads["w_out"] = _mm("dw_out", merged, dz1, "tn")

    def merge_bwd_fn(dm, gm, gl, a, b):
        sm, sl = _sigmoid(gm), _sigmoid(gl)
        return dm * sm, dm * sl, dm * a * sm * (1.0 - sm), dm * b * sl * (1.0 - sl)

    do_mla, do_lru, dgm, dgl = _rows(
        "merge_bwd", merge_bwd_fn, t, tm,
        [(dmerged, W, 0), (proj, W, SEG_GM), (proj, W, SEG_GL), (o_mla, W, 0), (o_lru, W, 0)], [],
        [(W, BF16)] * 4)
    do = _mm("d_o", do_mla, wb["w_o_mla"], "nt")
    grads["w_o_mla"] = _mm("dw_o_mla", o, do_mla, "tn")
    dqn, dqr_pre = _flash_dq(proj, qr, kv, kr, do, o, lse)
    dkv, dkr_h = _flash_dkv(proj, qr, kv, kr, do, o, lse)
    dckvn = _mm("d_ckvn", dkv, wb["w_ukv"], "nt")
    grads["w_ukv"] = _mm("dw_ukv", ckvn, dkv, "tn")

    def rope_bwd_fn(dqr, dckvn, ckv, c, s1, s2, dkr_h, g):
        dkr = dkr_h[0]
        for h in range(1, N_HEADS):
            dkr = dkr + dkr_h[h]
        dckv, dg = _rms_bwd(ckv, g, dckvn)
        return _rope_heads(dqr, c, -s1, -s2), _rope_group(dkr, c, -s1, -s2), dckv, dg

    dqr, dkr, dckv, grads["g_ckv"] = _rope_bwd(rope_bwd_fn, t, tm, dqr_pre, dckvn, proj, cos_t, s1_t, s2_t, dkr_h, g_ckv)
    doli = _mm("d_oli", do_lru, wb["w_o_lru"], "nt")
    grads["w_o_lru"] = _mm("dw_o_lru", oli, do_lru, "tn")
    (dlx, dly, grads["w_conv_lru"], grads["b_conv_lru"], grads["w_rg"], grads["b_rg"], grads["w_ig"],
     grads["b_ig"], grads["lru_lambda"]) = _lru_bwd(proj, hl, doli, w_conv, b_conv, w_rg, b_rg, w_ig, b_ig, lam)
    dproj = jnp.concatenate([dqn, dqr, dlx, dly, dgm, dgl, dckv, dkr,
                             jnp.zeros((t, PROJ_W - 12928), BF16)], axis=1)
    dh1 = _mm("d_h1", dproj, w_in, "nt")
    grads["w_in"] = _unperm_dw_in(_mm("dw_in", h1, dproj, "tn"))

    def in_bwd_fn(dres, x, dh, g):
        dx, dg = _rms_bwd(x, g, dh)
        return dres + dx, dg

    grad_x, grads["g_pre_mix"] = _rows("bwd_pre_mix", in_bwd_fn, t, tm, [(dx1, W, 0), (x, W, 0), (dh1, W, 0)],
                                       [g_pre_mix], [(W, F32)], [(1, W)])
    return loss_acc[0, 0], grad_x, grads


def _rope_bwd(fn, t, tm, dqr_pre, dckvn, proj, cos_t, s1_t, s2_t, dkr_h, g_ckv):
    W = D_MODEL

    def body(dqr_ref, dck_ref, ckv_ref, c_ref, s1_ref, s2_ref, dkrh_ref, g_ref, o1, o2, o3, o4):
        r1, r2, r3, dg = fn(dqr_ref[...], dck_ref[...], ckv_ref[...], c_ref[...], s1_ref[...], s2_ref[...],
                            dkrh_ref[...], g_ref[...])
        o1[...] = r1.astype(o1.dtype)
        o2[...] = r2.astype(o2.dtype)
        o3[...] = r3.astype(o3.dtype)

        @pl.when(pl.program_id(0) == 0)
        def _():
            o4[...] = jnp.zeros_like(o4)

        o4[...] += dg

    rb = lambda w, cb=0: pl.BlockSpec((tm, w), lambda i, cb=cb: (i, cb))
    return pl.pallas_call(
        body, name="rope_bwd", grid=(t // tm,),
        in_specs=[rb(W), rb(512), rb(512, CKV_BLOCK), rb(LANES), rb(LANES), rb(LANES),
                  pl.BlockSpec((N_HEADS, tm, LANES), lambda i: (0, i, 0)),
                  pl.BlockSpec((1, 512), lambda i: (0, 0))],
        out_specs=[rb(W), rb(LANES), rb(512), pl.BlockSpec((1, 512), lambda i: (0, 0))],
        out_shape=[jax.ShapeDtypeStruct((t, W), BF16), jax.ShapeDtypeStruct((t, LANES), BF16),
                   jax.ShapeDtypeStruct((t, 512), BF16), jax.ShapeDtypeStruct((1, 512), F32)],
        compiler_params=_params(("arbitrary",)),
    )(dqr_pre, dckvn, proj, cos_t, s1_t, s2_t, dkr_h, g_ckv)


HBM = pl.BlockSpec(memory_space=pl.ANY)


def _me():
    return lax.axis_index("x"), lax.axis_index("y"), lax.axis_index("c")


def _gather_chips(flat):
    def body(src, out, send_sems, recv_sems, local_sem):
        x, y, c = _me()
        chips = [(1 - x, y), (x, 1 - y), (1 - x, 1 - y)]

        def copy(k, slot, to):
            return pltpu.make_async_remote_copy(
                src_ref=src, dst_ref=out.at[slot], send_sem=send_sems.at[k], recv_sem=recv_sems.at[k],
                device_id=to, device_id_type=MESH)

        mine = pltpu.make_async_copy(src, out.at[2 * x + y], local_sem)
        mine.start()
        sends = [copy(k, 2 * x + y, (px, py, c)) for k, (px, py) in enumerate(chips)]
        for cp in sends:
            cp.start()
        for k, (px, py) in enumerate(chips):
            copy(k, 2 * px + py, (px, py, c)).wait_recv()
        for cp in sends:
            cp.wait_send()
        mine.wait()

    return pl.pallas_call(
        body, name="gather_weights", in_specs=[HBM], out_specs=HBM,
        out_shape=jax.ShapeDtypeStruct((4,) + flat.shape, flat.dtype),
        scratch_shapes=[pltpu.SemaphoreType.DMA((3,)), pltpu.SemaphoreType.DMA((3,)), pltpu.SemaphoreType.DMA],
    )(flat)


def _exchange_all(name, parts, scatter):
    rows_shape = parts.shape[1:] if scatter else parts.shape

    def body(src, out, send_sems, recv_sems, local_sem):
        x, y, c = _me()
        me = 4 * x + 2 * y + c

        def copy(q):
            return pltpu.make_async_remote_copy(
                src_ref=src.at[q] if scatter else src, dst_ref=out.at[me],
                send_sem=send_sems.at[q], recv_sem=recv_sems.at[me],
                device_id=(q // 4, (q // 2) % 2, q % 2), device_id_type=MESH)

        def arrival(s):
            return pltpu.make_async_remote_copy(
                src_ref=src.at[s] if scatter else src, dst_ref=out.at[s],
                send_sem=send_sems.at[s], recv_sem=recv_sems.at[s],
                device_id=(s // 4, (s // 2) % 2, s % 2), device_id_type=MESH)

        mine = pltpu.make_async_copy(src.at[me] if scatter else src, out.at[me], local_sem)
        mine.start()
        for q in range(8):
            @pl.when(me != q)
            def _(q=q):
                copy(q).start()
        for s in range(8):
            @pl.when(me != s)
            def _(s=s):
                arrival(s).wait_recv()
        for q in range(8):
            @pl.when(me != q)
            def _(q=q):
                copy(q).wait_send()
        mine.wait()

    return pl.pallas_call(
        body, name=name, in_specs=[HBM], out_specs=HBM,
        out_shape=jax.ShapeDtypeStruct((8,) + rows_shape, parts.dtype),
        scratch_shapes=[pltpu.SemaphoreType.DMA((8,)), pltpu.SemaphoreType.DMA((8,)), pltpu.SemaphoreType.DMA],
    )(parts)


def _gather_cores(piece):
    def body(src, out, send_sem, recv_sem, local_sem):
        x, y, c = _me()

        def copy(slot):
            return pltpu.make_async_remote_copy(
                src_ref=src, dst_ref=out.at[slot], send_sem=send_sem, recv_sem=recv_sem,
                device_id=(x, y, 1 - c), device_id_type=MESH)

        mine = pltpu.make_async_copy(src, out.at[c], local_sem)
        mine.start()
        send = copy(c)
        send.start()
        copy(1 - c).wait_recv()
        send.wait_send()
        mine.wait()

    return pl.pallas_call(
        body, name="gather_cores", in_specs=[HBM], out_specs=HBM,
        out_shape=jax.ShapeDtypeStruct((2,) + piece.shape, piece.dtype),
        scratch_shapes=[pltpu.SemaphoreType.DMA, pltpu.SemaphoreType.DMA, pltpu.SemaphoreType.DMA],
    )(piece)


def _sum_slots(name, buf, tr):
    _, r, w = buf.shape

    def body(b_ref, o_ref):
        acc = b_ref[0]
        for s in range(1, 8):
            acc = acc + b_ref[s]
        o_ref[...] = acc

    return pl.pallas_call(
        body, name=name, grid=(r // tr,),
        in_specs=[pl.BlockSpec((8, tr, w), lambda i: (0, i, 0))],
        out_specs=pl.BlockSpec((tr, w), lambda i: (i, 0)),
        out_shape=jax.ShapeDtypeStruct((r, w), F32),
        compiler_params=_params(("parallel",)),
    )(buf)


def _adamw(name, w, g, m, v, tr):
    def fn(w, g, m, v):
        m = ADAM_B1 * m + (1.0 - ADAM_B1) * g
        v = ADAM_B2 * v + (1.0 - ADAM_B2) * (g * g)
        m_hat = m / (1.0 - ADAM_B1 ** ADAM_STEP)
        v_hat = v / (1.0 - ADAM_B2 ** ADAM_STEP)
        delta = -ADAM_LR * (m_hat / (jnp.sqrt(v_hat) + ADAM_EPS) + ADAM_WD * w)
        return delta, m, v

    r = w.shape[0]
    return _rows(name, fn, r, tr, [(a, FLAT_W, 0) for a in (w, g, m, v)], [], [(FLAT_W, F32)] * 3)


def _pack(arrays, n_rows, dtype):
    flat = jnp.concatenate([a.reshape(-1).astype(dtype) for a in arrays])
    return jnp.pad(flat, (0, n_rows * FLAT_W - flat.shape[0])).reshape(n_rows, FLAT_W)


def _unpack(flat, shapes):
    flat = flat.reshape(-1)
    out, off = [], 0
    for s in shapes:
        n = math.prod(s)
        out.append(flat[off:off + n].reshape(s))
        off += n
    return out


def kernel(x, mem, positions, g_pre_mix, g_post_mix, w_in, g_ckv, w_ukv, w_o_mla, w_conv_lru, b_conv_lru, w_rg, b_rg, w_ig, b_ig, lru_lambda, w_o_lru, w_out, g_pre_x, g_post_x, g_mem, w_cq, w_ck, w_cv, w_co, g_pre_ffn, g_post_ffn, w_up, w_fconv, b_fconv, w_down, loss_target, m_g_pre_mix, m_g_post_mix, m_w_in, m_g_ckv, m_w_ukv, m_w_o_mla, m_w_conv_lru, m_b_conv_lru, m_w_rg, m_b_rg, m_w_ig, m_b_ig, m_lru_lambda, m_w_o_lru, m_w_out, m_g_pre_x, m_g_post_x, m_g_mem, m_w_cq, m_w_ck, m_w_cv, m_w_co, m_g_pre_ffn, m_g_post_ffn, m_w_up, m_w_fconv, m_b_fconv, m_w_down, v_g_pre_mix, v_g_post_mix, v_w_in, v_g_ckv, v_w_ukv, v_w_o_mla, v_w_conv_lru, v_b_conv_lru, v_w_rg, v_b_rg, v_w_ig, v_b_ig, v_lru_lambda, v_w_o_lru, v_w_out, v_g_pre_x, v_g_post_x, v_g_mem, v_w_cq, v_w_ck, v_w_cv, v_w_co, v_g_pre_ffn, v_g_post_ffn, v_w_up, v_w_fconv, v_b_fconv, v_w_down):
    given = dict(locals())
    big_names = [n for n, _, _ in BIG]
    big_shapes = [s for _, s, _ in BIG]
    small_names = [n for n, _ in SMALL]
    small_shapes = [s for _, s in SMALL]

    w_flat = _pack([given[n][0] for n in big_names], SHARD_ROWS, BF16)
    w_all = _gather_chips(w_flat)
    shards = [_unpack(w_all[j], big_shapes) for j in range(4)]
    wb = {n: jnp.concatenate([shards[j][i] for j in range(4)], axis=ax) for i, (n, _, ax) in enumerate(BIG)}
    ws = {n: given[n][0] for n in small_names}

    loss_part, grad_x, grads = _local_step(x[0], mem[0], positions[0], loss_target[0], wb, ws)
    loss = lax.psum(loss_part, ("x", "y", "c"))

    per_owner = []
    for j in range(4):
        per_owner.append(_pack([jnp.split(grads[n], 4, axis=ax)[j] for n, _, ax in BIG], SHARD_ROWS, F32))
    parts = jnp.stack(per_owner).reshape(8, PIECE_ROWS, FLAT_W)
    landed = _exchange_all("scatter_grads", parts, scatter=True)
    piece = _sum_slots("reduce_grads", landed, 256)
    g_big = _gather_cores(piece).reshape(SHARD_ROWS, FLAT_W)

    s_part = _pack([grads[n].reshape(s) for n, s in SMALL], SMALL_ROWS, F32)
    g_small = _sum_slots("reduce_small", _exchange_all("gather_small", s_part, scatter=False), 184)

    pack_big = lambda pre: _pack([given[pre + n][0] for n in big_names], SHARD_ROWS, F32)
    pack_small = lambda pre: _pack([given[pre + n][0] for n in small_names], SMALL_ROWS, F32)
    big_out = (g_big,) + tuple(_adamw("adamw_big", pack_big(""), g_big, pack_big("m_"), pack_big("v_"), 256))
    small_out = (g_small,) + tuple(_adamw("adamw_small", pack_small(""), g_small, pack_small("m_"), pack_small("v_"), 184))

    outs = []
    for kind in range(4):
        got = dict(zip(big_names, _unpack(big_out[kind], big_shapes)))
        got.update(zip(small_names, _unpack(small_out[kind], small_shapes)))
        outs.extend(got[n][None] for n in WEIGHTS)
    return (loss, grad_x[None], *outs)
```

```python
import functools
import math

import jax
import jax.numpy as jnp
from jax import lax
from jax.experimental import pallas as pl
from jax.experimental.pallas import tpu as pltpu

F32 = jnp.float32
BF16 = jnp.bfloat16
MESH = pl.DeviceIdType.MESH

EPS = 1e-6
D_MODEL = 2048
N_HEADS = 16
HEAD_W = 128
ROPE_HALF = 32
SOFTMAX_SCALE = 192 ** -0.5
ROPE_THETA = 10000.0
LRU_BLOCKS = 16
LRU_C = 8.0
X_HEADS = 4
X_SCALE = 128 ** -0.5
D_FF = 5632

ADAM_LR = 0.001
ADAM_B1 = 0.9
ADAM_B2 = 0.999
ADAM_EPS = 1e-08
ADAM_WD = 0.01
ADAM_STEP = 10

V7X_VMEM_LIMIT = 56 * 1024 * 1024
LANES = 128
SUBLANES = 8

PROJ_W = 13312
SEG_QN, SEG_QR, SEG_LX, SEG_LY, SEG_GM, SEG_GL = range(6)
CKV_BLOCK = 24
KR_BLOCK = 100

BIG = (
    ("w_in", (2048, 2960), 1), ("w_ukv", (512, 1024), 1), ("w_o_mla", (512, 2048), 0),
    ("w_o_lru", (512, 2048), 0), ("w_out", (512, 2048), 0),
    ("w_cq", (512, 512), 0), ("w_ck", (512, 512), 0), ("w_cv", (512, 512), 0),
    ("w_co", (512, 512), 1), ("w_up", (2048, 2816), 1), ("w_down", (1408, 2048), 0),
)
CONV = (("w_conv_lru", (4, 512), 1), ("w_fconv", (3, 2816), 1))
SMALL = (
    ("g_pre_mix", (2048,)), ("g_post_mix", (2048,)), ("g_ckv", (512,)), ("b_conv_lru", (2048,)),
    ("w_rg", (16, 128, 128)), ("b_rg", (16, 128)), ("w_ig", (16, 128, 128)), ("b_ig", (16, 128)),
    ("lru_lambda", (2048,)), ("g_pre_x", (2048,)), ("g_post_x", (2048,)), ("g_mem", (2048,)),
    ("g_pre_ffn", (2048,)), ("g_post_ffn", (2048,)), ("b_fconv", (11264,)),
)
WEIGHTS = ("g_pre_mix", "g_post_mix", "w_in", "g_ckv", "w_ukv", "w_o_mla", "w_conv_lru", "b_conv_lru",
           "w_rg", "b_rg", "w_ig", "b_ig", "lru_lambda", "w_o_lru", "w_out", "g_pre_x", "g_post_x", "g_mem",
           "w_cq", "w_ck", "w_cv", "w_co", "g_pre_ffn", "g_post_ffn", "w_up", "w_fconv", "b_fconv", "w_down")
FLAT_W = 1024
SMALL_XCHG_ROWS = 640
SMALL_ROWS = 560


def _params(sem=None):
    return pltpu.CompilerParams(dimension_semantics=sem, vmem_limit_bytes=V7X_VMEM_LIMIT)


def _tile(dim, pref):
    if dim <= pref:
        return dim
    for t in range(pref, 0, -LANES):
        if dim % t == 0:
            return t
    raise ValueError((dim, pref))


def _mm(name, a, b, mode, out_dtype=F32, tm=1024, tn=1024, tk=512, owners=1):
    if mode == "nn":
        (m, k), (_, n) = a.shape, b.shape
    elif mode == "nt":
        (m, k), (n, _) = a.shape, b.shape
    else:
        (k, m), (_, n) = a.shape, b.shape
    tm, tn, tk = _tile(m, tm), _tile(n // owners, tn), _tile(k, tk)
    nk = k // tk
    per_owner = n // owners // tn
    if owners == 1:
        out_spec = pl.BlockSpec((tm, tn), lambda i, j, kk: (i, j))
        out_shape = jax.ShapeDtypeStruct((m, n), out_dtype)
    else:
        out_spec = pl.BlockSpec((None, tm, tn), lambda i, j, kk: (j // per_owner, i, j % per_owner))
        out_shape = jax.ShapeDtypeStruct((owners, m, n // owners), out_dtype)
    if mode == "nn":
        a_spec = pl.BlockSpec((tm, tk), lambda i, j, kk: (i, kk))
        b_spec = pl.BlockSpec((tk, tn), lambda i, j, kk: (kk, j))
        dims = (((1,), (0,)), ((), ()))
    elif mode == "nt":
        a_spec = pl.BlockSpec((tm, tk), lambda i, j, kk: (i, kk))
        b_spec = pl.BlockSpec((tn, tk), lambda i, j, kk: (j, kk))
        dims = (((1,), (1,)), ((), ()))
    else:
        a_spec = pl.BlockSpec((tk, tm), lambda i, j, kk: (kk, i))
        b_spec = pl.BlockSpec((tk, tn), lambda i, j, kk: (kk, j))
        dims = (((0,), (0,)), ((), ()))

    def body(a_ref, b_ref, o_ref, acc_ref):
        kk = pl.program_id(2)

        @pl.when(kk == 0)
        def _():
            acc_ref[...] = jnp.zeros_like(acc_ref)

        acc_ref[...] += lax.dot_general(a_ref[...].astype(BF16), b_ref[...].astype(BF16), dims,
                                        preferred_element_type=F32)

        @pl.when(kk == nk - 1)
        def _():
            o_ref[...] = acc_ref[...].astype(o_ref.dtype)

    return pl.pallas_call(
        body, name=name, grid=(m // tm, n // tn, nk),
        in_specs=[a_spec, b_spec], out_specs=out_spec, out_shape=out_shape,
        scratch_shapes=[pltpu.VMEM((tm, tn), F32)],
        compiler_params=_params(("parallel", "parallel", "arbitrary")),
    )(a, b)


def _rows(name, fn, n_rows, tm, row_ins, full_ins, row_outs, acc_outs=()):
    in_specs, args = [], []
    for arr, width, cb in row_ins:
        in_specs.append(pl.BlockSpec((tm, width), lambda i, cb=cb: (i, cb)))
        args.append(arr)
    for arr in full_ins:
        in_specs.append(pl.BlockSpec(arr.shape, lambda i, nd=arr.ndim: (0,) * nd))
        args.append(arr)
    out_shape = [jax.ShapeDtypeStruct((n_rows, w), dt) for w, dt in row_outs]
    out_specs = [pl.BlockSpec((tm, w), lambda i: (i, 0)) for w, _ in row_outs]
    for s in acc_outs:
        out_shape.append(jax.ShapeDtypeStruct(s, F32))
        out_specs.append(pl.BlockSpec(s, lambda i, nd=len(s): (0,) * nd))
    n_in, n_ro = len(args), len(row_outs)

    def body(*refs):
        res = fn(*[r[...] for r in refs[:n_in]])
        outs = refs[n_in:]
        for r, v in zip(outs[:n_ro], res[:n_ro]):
            r[...] = v.astype(r.dtype)
        first = pl.program_id(0) == 0
        for r, v in zip(outs[n_ro:], res[n_ro:]):
            @pl.when(first)
            def _(r=r):
                r[...] = jnp.zeros_like(r)

            r[...] += v

    return pl.pallas_call(
        body, name=name, grid=(n_rows // tm,), in_specs=in_specs, out_specs=out_specs, out_shape=out_shape,
        compiler_params=_params(("arbitrary",)),
    )(*args)


def _rsq(x):
    return lax.rsqrt(jnp.mean(x * x, axis=-1, keepdims=True) + EPS)


def _rms(x, g):
    return x * _rsq(x) * g


def _rms_bwd(x, g, dy):
    r = _rsq(x)
    xh = x * r
    dxh = dy * g
    dx = r * (dxh - xh * jnp.mean(dxh * xh, axis=-1, keepdims=True))
    return dx, jnp.sum(dy * xh, axis=0, keepdims=True)


def _sigmoid(x):
    return 1.0 / (1.0 + jnp.exp(-x))


_GELU_C = math.sqrt(2.0 / math.pi)


def _gelu(x):
    return 0.5 * x * (1.0 + jnp.tanh(_GELU_C * (x + 0.044715 * x * x * x)))


def _gelu_and_grad(x):
    th = jnp.tanh(_GELU_C * (x + 0.044715 * x * x * x))
    g = 0.5 * x * (1.0 + th)
    dg = 0.5 * (1.0 + th) + 0.5 * x * (1.0 - th * th) * _GELU_C * (1.0 + 3.0 * 0.044715 * x * x)
    return g, dg


def _dot(a, b):
    return jnp.dot(a.astype(BF16), b.astype(BF16), preferred_element_type=F32)


def _dot_nt(a, b):
    return lax.dot_general(a.astype(BF16), b.astype(BF16), (((1,), (1,)), ((), ())), preferred_element_type=F32)


def _dot_tn(a, b):
    return lax.dot_general(a.astype(BF16), b.astype(BF16), (((0,), (0,)), ((), ())), preferred_element_type=F32)


def _row_iota(shape):
    return lax.broadcasted_iota(jnp.int32, shape, 0)


def _shift_down(x, k, halo):
    xs = pltpu.roll(x, k, 0)
    hs = pltpu.roll(halo, k, 0)
    first = jnp.where(_row_iota(hs.shape) < k, hs, xs[:SUBLANES])
    return jnp.concatenate([first, xs[SUBLANES:]], axis=0)


def _shift_up(x, k, halo):
    tm = x.shape[0]
    xs = pltpu.roll(x, tm - k, 0)
    hs = pltpu.roll(halo, SUBLANES - k, 0)
    last = jnp.where(_row_iota(hs.shape) >= SUBLANES - k, hs, xs[tm - SUBLANES:])
    return jnp.concatenate([xs[:tm - SUBLANES], last], axis=0)


def _conv_taps(x, halo, n_taps):
    return [x] + [_shift_down(x, k, halo) for k in range(1, n_taps)]


def _causal_conv(taps, w, b):
    kw = len(taps)
    y = b + w[kw - 1:kw] * taps[0]
    for s in range(1, kw):
        y = y + w[kw - 1 - s:kw - s] * taps[s]
    return y


def _rope_group(g, c, s1, s2):
    return g * c + pltpu.roll(g, LANES - ROPE_HALF, 1) * s1 + pltpu.roll(g, ROPE_HALF, 1) * s2


def _rope_heads(q, c, s1, s2):
    return jnp.concatenate(
        [_rope_group(q[:, h * HEAD_W:(h + 1) * HEAD_W], c, s1, s2) for h in range(N_HEADS)], axis=1)


def _rope_tables(pos_col, tm):
    inv_freq = ROPE_THETA ** (-jnp.arange(0, 2 * ROPE_HALF, 2, dtype=F32) / (2 * ROPE_HALF))
    invf = jnp.concatenate([inv_freq, inv_freq, jnp.zeros((LANES - 2 * ROPE_HALF,), F32)])[None, :]

    def fn(pos, invf):
        ang = pos.astype(F32) * invf
        c, s = jnp.cos(ang), jnp.sin(ang)
        lane = lax.broadcasted_iota(jnp.int32, ang.shape, 1)
        s1 = jnp.where(lane < ROPE_HALF, -s, 0.0)
        s2 = jnp.where((lane >= ROPE_HALF) & (lane < 2 * ROPE_HALF), s, 0.0)
        return c, s1, s2

    n = pos_col.shape[0]
    return _rows("rope_tables", fn, n, tm, [(pos_col, 1, 0)], [invf], [(LANES, F32)] * 3)


def _attn_tiles(t):
    tq = min(512, t)
    nq = t // tq
    assert nq == 1 or nq % 2 == 0
    return tq, nq, max(nq // 2, 1), (nq + 1 if nq > 1 else 1)


def _tri_q(p, s, nq):
    first = s <= p
    return jnp.where(first, p, nq - 1 - p), jnp.where(first, s, s - p - 1)


def _tri_k(p, s, nq):
    first = s < nq - p
    return jnp.where(first, p + s, s - 1), jnp.where(first, p, nq - 1 - p)


def _scores(qn_ref, qr_ref, kv_ref, kr_ref, diag):
    q = jnp.concatenate([qn_ref[...].astype(BF16), qr_ref[...]], axis=1)
    k = jnp.concatenate([kv_ref[:, :HEAD_W], kr_ref[...]], axis=1)
    s = _dot_nt(q, k) * SOFTMAX_SCALE
    if diag:
        row = lax.broadcasted_iota(jnp.int32, s.shape, 0)
        col = lax.broadcasted_iota(jnp.int32, s.shape, 1)
        s = jnp.where(col <= row, s, -1e30)
    return s, q, k


def _flash_fwd(proj, qr, kv, kr):
    t = proj.shape[0]
    tq, nq, n_pairs, n_steps = _attn_tiles(t)

    def body(qn_ref, qr_ref, kv_ref, kr_ref, o_ref, lse_ref, m_s, l_s, acc_s):
        qi, ki = _tri_q(pl.program_id(1), pl.program_id(2), nq)

        @pl.when(ki == 0)
        def _():
            m_s[...] = jnp.full_like(m_s, -1e30)
            l_s[...] = jnp.zeros_like(l_s)
            acc_s[...] = jnp.zeros_like(acc_s)

        def step(diag):
            s, _, _ = _scores(qn_ref, qr_ref, kv_ref, kr_ref, diag)
            m_prev = m_s[...]
            m_new = jnp.maximum(m_prev, jnp.max(s, axis=1, keepdims=True))
            alpha = jnp.exp(m_prev - m_new)
            p = jnp.exp(s - m_new)
            l_s[...] = alpha * l_s[...] + jnp.sum(p, axis=1, keepdims=True)
            acc_s[...] = alpha * acc_s[...] + _dot(p, kv_ref[:, HEAD_W:])
            m_s[...] = m_new

        @pl.when(ki < qi)
        def _():
            step(False)

        @pl.when(ki == qi)
        def _():
            step(True)
            o_ref[...] = acc_s[...] / l_s[...]
            lse_ref[...] = m_s[...] + jnp.log(l_s[...])

    qb = lambda p, s: _tri_q(p, s, nq)[0]
    kb = lambda p, s: _tri_q(p, s, nq)[1]
    return pl.pallas_call(
        body, name="mla_fwd", grid=(N_HEADS, n_pairs, n_steps),
        in_specs=[pl.BlockSpec((tq, HEAD_W), lambda h, p, s: (qb(p, s), h)),
                  pl.BlockSpec((tq, HEAD_W), lambda h, p, s: (qb(p, s), h)),
                  pl.BlockSpec((tq, 2 * HEAD_W), lambda h, p, s: (kb(p, s), h)),
                  pl.BlockSpec((tq, HEAD_W), lambda h, p, s: (kb(p, s), 0))],
        out_specs=[pl.BlockSpec((tq, HEAD_W), lambda h, p, s: (qb(p, s), h)),
                   pl.BlockSpec((None, tq, 1), lambda h, p, s: (h, qb(p, s), 0))],
        out_shape=[jax.ShapeDtypeStruct((t, N_HEADS * HEAD_W), F32),
                   jax.ShapeDtypeStruct((N_HEADS, t, 1), F32)],
        scratch_shapes=[pltpu.VMEM((tq, 1), F32), pltpu.VMEM((tq, 1), F32), pltpu.VMEM((tq, HEAD_W), F32)],
        compiler_params=_params(("parallel", "parallel", "arbitrary")),
    )(proj, qr, kv, kr)


def _flash_dq(proj, qr, kv, kr, do, o, lse):
    t = proj.shape[0]
    tq, nq, n_pairs, n_steps = _attn_tiles(t)

    def body(qn_ref, qr_ref, kv_ref, kr_ref, do_ref, o_ref, lse_ref, dqn_ref, dqr_ref, dq_s, dsum_s):
        qi, ki = _tri_q(pl.program_id(1), pl.program_id(2), nq)

        @pl.when(ki == 0)
        def _():
            dq_s[...] = jnp.zeros_like(dq_s)
            dsum_s[...] = jnp.sum(do_ref[...] * o_ref[...], axis=1, keepdims=True)

        def step(diag):
            s, _, k = _scores(qn_ref, qr_ref, kv_ref, kr_ref, diag)
            p = jnp.exp(s - lse_ref[...])
            dp = _dot_nt(do_ref[...], kv_ref[:, HEAD_W:])
            ds = p * (dp - dsum_s[...]) * SOFTMAX_SCALE
            dq_s[...] += _dot(ds, k)

        @pl.when(ki < qi)
        def _():
            step(False)

        @pl.when(ki == qi)
        def _():
            step(True)
            dqn_ref[...] = dq_s[:, :HEAD_W].astype(dqn_ref.dtype)
            dqr_ref[...] = dq_s[:, HEAD_W:]

    qb = lambda p, s: _tri_q(p, s, nq)[0]
    kb = lambda p, s: _tri_q(p, s, nq)[1]
    qspec = pl.BlockSpec((tq, HEAD_W), lambda h, p, s: (qb(p, s), h))
    return pl.pallas_call(
        body, name="mla_dq", grid=(N_HEADS, n_pairs, n_steps),
        in_specs=[qspec, qspec,
                  pl.BlockSpec((tq, 2 * HEAD_W), lambda h, p, s: (kb(p, s), h)),
                  pl.BlockSpec((tq, HEAD_W), lambda h, p, s: (kb(p, s), 0)),
                  qspec, qspec,
                  pl.BlockSpec((None, tq, 1), lambda h, p, s: (h, qb(p, s), 0))],
        out_specs=[qspec, qspec],
        out_shape=[jax.ShapeDtypeStruct((t, N_HEADS * HEAD_W), BF16),
                   jax.ShapeDtypeStruct((t, N_HEADS * HEAD_W), F32)],
        scratch_shapes=[pltpu.VMEM((tq, 2 * HEAD_W), F32), pltpu.VMEM((tq, 1), F32)],
        compiler_params=_params(("parallel", "parallel", "arbitrary")),
    )(proj, qr, kv, kr, do, o, lse)


def _flash_dkv(proj, qr, kv, kr, do, o, lse):
    t = proj.shape[0]
    tq, nq, n_pairs, n_steps = _attn_tiles(t)

    def body(qn_ref, qr_ref, kv_ref, kr_ref, do_ref, o_ref, lse_ref, dkv_ref, dkr_ref, dk_s, dv_s):
        qi, ki = _tri_k(pl.program_id(1), pl.program_id(2), nq)

        @pl.when(qi == ki)
        def _():
            dk_s[...] = jnp.zeros_like(dk_s)
            dv_s[...] = jnp.zeros_like(dv_s)

        def step(diag):
            s, q, _ = _scores(qn_ref, qr_ref, kv_ref, kr_ref, diag)
            p = jnp.exp(s - lse_ref[...])
            do = do_ref[...]
            dsum = jnp.sum(do * o_ref[...], axis=1, keepdims=True)
            dv_s[...] += _dot_tn(p, do)
            dp = _dot_nt(do, kv_ref[:, HEAD_W:])
            ds = p * (dp - dsum) * SOFTMAX_SCALE
            dk_s[...] += _dot_tn(ds, q)

        @pl.when(qi == ki)
        def _():
            step(True)

        @pl.when(qi > ki)
        def _():
            step(False)

        @pl.when(qi == nq - 1)
        def _():
            dkv_ref[...] = jnp.concatenate([dk_s[:, :HEAD_W], dv_s[...]], axis=1).astype(dkv_ref.dtype)
            dkr_ref[...] = dk_s[:, HEAD_W:]

    qb = lambda p, s: _tri_k(p, s, nq)[0]
    kb = lambda p, s: _tri_k(p, s, nq)[1]
    qspec = pl.BlockSpec((tq, HEAD_W), lambda h, p, s: (qb(p, s), h))
    return pl.pallas_call(
        body, name="mla_dkv", grid=(N_HEADS, n_pairs, n_steps),
        in_specs=[qspec, qspec,
                  pl.BlockSpec((tq, 2 * HEAD_W), lambda h, p, s: (kb(p, s), h)),
                  pl.BlockSpec((tq, HEAD_W), lambda h, p, s: (kb(p, s), 0)),
                  qspec, qspec,
                  pl.BlockSpec((None, tq, 1), lambda h, p, s: (h, qb(p, s), 0))],
        out_specs=[pl.BlockSpec((tq, 2 * HEAD_W), lambda h, p, s: (kb(p, s), h)),
                   pl.BlockSpec((None, tq, HEAD_W), lambda h, p, s: (h, kb(p, s), 0))],
        out_shape=[jax.ShapeDtypeStruct((t, N_HEADS * 2 * HEAD_W), BF16),
                   jax.ShapeDtypeStruct((N_HEADS, t, HEAD_W), F32)],
        scratch_shapes=[pltpu.VMEM((tq, 2 * HEAD_W), F32), pltpu.VMEM((tq, HEAD_W), F32)],
        compiler_params=_params(("parallel", "parallel", "arbitrary")),
    )(proj, qr, kv, kr, do, o, lse)


def _block_diag(x, w):
    return jnp.concatenate(
        [_dot(x[:, b * LANES:(b + 1) * LANES], w[b]) for b in range(LRU_BLOCKS)], axis=1)


def _block_diag_t(d, w):
    return jnp.concatenate(
        [_dot_nt(d[:, b * LANES:(b + 1) * LANES], w[b]) for b in range(LRU_BLOCKS)], axis=1)


def _lru_gates(xc, w_rg, b_rg, w_ig, b_ig, lam):
    r = _sigmoid(_block_diag(xc, w_rg) + b_rg)
    ig = _sigmoid(_block_diag(xc, w_ig) + b_ig)
    sp = jnp.maximum(-lam, 0.0) + jnp.log1p(jnp.exp(-jnp.abs(lam)))
    log_a = -LRU_C * r * sp
    a = jnp.exp(log_a)
    mult = jnp.sqrt(-jnp.tanh(log_a) * (a * a + 1.0))
    return r, ig, sp, a, mult


def _lru_tm(t):
    return min(128, t)


def _halo_spec(tm, width, cb):
    return pl.BlockSpec((SUBLANES, width), lambda i, cb=cb: (jnp.maximum(i * (tm // SUBLANES) - 1, 0), cb))


def _lru_fwd(proj, w_conv, b_conv, w_rg, b_rg, w_ig, b_ig, lam):
    t = proj.shape[0]
    tm = _lru_tm(t)
    w = D_MODEL
    n_scan = int(math.log2(tm))

    def body(lx_ref, halo_ref, ly_ref, wc_ref, bc_ref, wrg_ref, brg_ref, wig_ref, big_ref, lam_ref,
             oli_ref, h_ref, carry_s):
        i = pl.program_id(0)

        @pl.when(i == 0)
        def _():
            carry_s[...] = jnp.zeros_like(carry_s)

        x = lx_ref[...]
        halo = jnp.where(i > 0, halo_ref[...], 0.0)
        xc = _causal_conv(_conv_taps(x, halo, 4), wc_ref[...], bc_ref[...])
        _, ig, _, a, mult = _lru_gates(xc, wrg_ref[...], brg_ref[...], wig_ref[...], big_ref[...], lam_ref[...])
        u = mult * (ig * xc)
        rows = _row_iota(a.shape)
        for s in range(n_scan):
            d = 1 << s
            keep = rows >= d
            a_s = jnp.where(keep, pltpu.roll(a, d, 0), 1.0)
            u_s = jnp.where(keep, pltpu.roll(u, d, 0), 0.0)
            u = a * u_s + u
            a = a * a_s
        h = u + a * carry_s[SUBLANES - 1:SUBLANES, :]
        h_ref[...] = h
        carry_s[...] = h[tm - SUBLANES:, :]
        oli_ref[...] = (h * _gelu(ly_ref[...])).astype(oli_ref.dtype)

    full = lambda arr: pl.BlockSpec(arr.shape, lambda i, nd=arr.ndim: (0,) * nd)
    return pl.pallas_call(
        body, name="lru_fwd", grid=(t // tm,),
        in_specs=[pl.BlockSpec((tm, w), lambda i: (i, SEG_LX)), _halo_spec(tm, w, SEG_LX),
                  pl.BlockSpec((tm, w), lambda i: (i, SEG_LY)),
                  full(w_conv), full(b_conv), full(w_rg), full(b_rg), full(w_ig), full(b_ig), full(lam)],
        out_specs=[pl.BlockSpec((tm, w), lambda i: (i, 0)), pl.BlockSpec((tm, w), lambda i: (i, 0))],
        out_shape=[jax.ShapeDtypeStruct((t, w), BF16), jax.ShapeDtypeStruct((t, w), F32)],
        scratch_shapes=[pltpu.VMEM((SUBLANES, w), F32)],
        compiler_params=_params(("arbitrary",)),
    )(proj, proj, proj, w_conv, b_conv, w_rg, b_rg, w_ig, b_ig, lam)


def _lru_bwd(proj, hl, doli, w_conv, b_conv, w_rg, b_rg, w_ig, b_ig, lam):
    t = proj.shape[0]
    tm = _lru_tm(t)
    nt = t // tm
    w = D_MODEL
    n_scan = int(math.log2(tm))

    def body(lx_ref, lxh_ref, ly_ref, h_ref, hh_ref, doli_ref,
             wc_ref, bc_ref, wrg_ref, brg_ref, wig_ref, big_ref, lam_ref,
             dlx_ref, dly_ref, dwc_ref, dbc_ref, dwrg_ref, dbrg_ref, dwig_ref, dbig_ref, dlam_ref,
             ca_s, cd_s, cx_s):
        i = pl.program_id(0)
        blk = nt - 1 - i

        @pl.when(i == 0)
        def _():
            for r in (ca_s, cd_s, cx_s, dwc_ref, dbc_ref, dwrg_ref, dbrg_ref, dwig_ref, dbig_ref, dlam_ref):
                r[...] = jnp.zeros_like(r)

        x = lx_ref[...]
        halo = jnp.where(blk > 0, lxh_ref[...], 0.0)
        taps = _conv_taps(x, halo, 4)
        wc = wc_ref[...]
        xc = _causal_conv(taps, wc, bc_ref[...])
        w_rg, w_ig, lam_v = wrg_ref[...], wig_ref[...], lam_ref[...]
        r, ig, sp, a, mult = _lru_gates(xc, w_rg, brg_ref[...], w_ig, big_ref[...], lam_v)
        h = h_ref[...]
        h_prev = _shift_down(h, 1, jnp.where(blk > 0, hh_ref[...], 0.0))
        gl, dgl = _gelu_and_grad(ly_ref[...])
        doli = doli_ref[...]
        dly_ref[...] = (doli * h * dgl).astype(dly_ref.dtype)
        acc_b = doli * gl
        acc_a = _shift_up(a, 1, ca_s[...])
        rows = _row_iota(a.shape)
        for s in range(n_scan):
            d = 1 << s
            keep = rows < tm - d
            a_s = jnp.where(keep, pltpu.roll(acc_a, tm - d, 0), 1.0)
            b_s = jnp.where(keep, pltpu.roll(acc_b, tm - d, 0), 0.0)
            acc_b = acc_b + acc_a * b_s
            acc_a = acc_a * a_s
        dht = acc_b + acc_a * cd_s[0:1, :]
        da = dht * h_prev
        dmult = dht * (ig * xc)
        di = dht * mult * xc
        dxc = dht * mult * ig
        dlog_a = da * a - dmult * (a * a) / mult
        dr = dlog_a * (-LRU_C * sp)
        dlam_ref[...] += jnp.sum(dlog_a * r, axis=0, keepdims=True) * (LRU_C * _sigmoid(-lam_v))
        dpr = dr * r * (1.0 - r)
        dpi = di * ig * (1.0 - ig)
        dxc = dxc + _block_diag_t(dpr, w_rg) + _block_diag_t(dpi, w_ig)
        dbrg_ref[...] += jnp.sum(dpr, axis=0, keepdims=True)
        dbig_ref[...] += jnp.sum(dpi, axis=0, keepdims=True)
        for b in range(LRU_BLOCKS):
            sl = slice(b * LANES, (b + 1) * LANES)
            dwrg_ref[b] += _dot_tn(xc[:, sl], dpr[:, sl])
            dwig_ref[b] += _dot_tn(xc[:, sl], dpi[:, sl])
        dbc_ref[...] += jnp.sum(dxc, axis=0, keepdims=True)
        dwc_ref[...] += jnp.concatenate(
            [jnp.sum(dxc * taps[3 - k], axis=0, keepdims=True) for k in range(4)], axis=0)
        cx = cx_s[...]
        dlx = wc[3:4] * dxc
        for s in range(1, 4):
            dlx = dlx + wc[3 - s:4 - s] * _shift_up(dxc, s, cx)
        dlx_ref[...] = dlx.astype(dlx_ref.dtype)
        ca_s[...] = a[:SUBLANES, :]
        cd_s[...] = dht[:SUBLANES, :]
        cx_s[...] = dxc[:SUBLANES, :]

    full = lambda arr: pl.BlockSpec(arr.shape, lambda i, nd=arr.ndim: (0,) * nd)
    rev = lambda cb: pl.BlockSpec((tm, w), lambda i, cb=cb: (nt - 1 - i, cb))
    halo = lambda cb: pl.BlockSpec(
        (SUBLANES, w), lambda i, cb=cb: (jnp.maximum((nt - 1 - i) * (tm // SUBLANES) - 1, 0), cb))
    acc = lambda s: pl.BlockSpec(s, lambda i, nd=len(s): (0,) * nd)
    acc_shapes = [(4, w), (1, w), (LRU_BLOCKS, LANES, LANES), (1, w), (LRU_BLOCKS, LANES, LANES), (1, w), (1, w)]
    return pl.pallas_call(
        body, name="lru_bwd", grid=(nt,),
        in_specs=[rev(SEG_LX), halo(SEG_LX), rev(SEG_LY), rev(0), halo(0), rev(0),
                  full(w_conv), full(b_conv), full(w_rg), full(b_rg), full(w_ig), full(b_ig), full(lam)],
        out_specs=[rev(0), rev(0)] + [acc(s) for s in acc_shapes],
        out_shape=[jax.ShapeDtypeStruct((t, w), BF16), jax.ShapeDtypeStruct((t, w), BF16)]
        + [jax.ShapeDtypeStruct(s, F32) for s in acc_shapes],
        scratch_shapes=[pltpu.VMEM((SUBLANES, w), F32)] * 3,
        compiler_params=_params(("arbitrary",)),
    )(proj, proj, proj, hl, hl, doli, w_conv, b_conv, w_rg, b_rg, w_ig, b_ig, lam)


FFN_TC = 512
FFN_NJ = D_FF // FFN_TC


def _ffn_tm(t):
    return min(512, t)


def _ffn_fwd(up_pre, w_fconv, b_fconv):
    t = up_pre.shape[0]
    tm = _ffn_tm(t)

    def body(g_ref, gh_ref, v_ref, vh_ref, wg_ref, wv_ref, bg_ref, bv_ref, act_ref):
        first = pl.program_id(0) > 0
        gate = _causal_conv(_conv_taps(g_ref[...], jnp.where(first, gh_ref[...], 0.0), 3), wg_ref[...], bg_ref[...])
        val = _causal_conv(_conv_taps(v_ref[...], jnp.where(first, vh_ref[...], 0.0), 3), wv_ref[...], bv_ref[...])
        act_ref[...] = (_gelu(gate) * val).astype(act_ref.dtype)

    blk = lambda off: pl.BlockSpec((tm, FFN_TC), lambda i, j: (i, j + off))
    halo = lambda off: pl.BlockSpec(
        (SUBLANES, FFN_TC), lambda i, j: (jnp.maximum(i * (tm // SUBLANES) - 1, 0), j + off))
    wsp = lambda rows, off: pl.BlockSpec((rows, FFN_TC), lambda i, j: (0, j + off))
    return pl.pallas_call(
        body, name="ffn_conv_fwd", grid=(t // tm, FFN_NJ),
        in_specs=[blk(0), halo(0), blk(FFN_NJ), halo(FFN_NJ), wsp(3, 0), wsp(3, FFN_NJ), wsp(1, 0), wsp(1, FFN_NJ)],
        out_specs=pl.BlockSpec((tm, FFN_TC), lambda i, j: (i, j)),
        out_shape=jax.ShapeDtypeStruct((t, D_FF), BF16),
        compiler_params=_params(("parallel", "parallel")),
    )(up_pre, up_pre, up_pre, up_pre, w_fconv, w_fconv, b_fconv, b_fconv)


def _ffn_bwd_act(up_pre, dact, w_fconv, b_fconv):
    t = up_pre.shape[0]
    tm = _ffn_tm(t)

    def body(g_ref, gh_ref, v_ref, vh_ref, da_ref, wg_ref, wv_ref, bg_ref, bv_ref, dup_ref, dw_ref, db_ref):
        jj, i = pl.program_id(0), pl.program_id(1)

        @pl.when(i == 0)
        def _():
            dw_ref[...] = jnp.zeros_like(dw_ref)
            db_ref[...] = jnp.zeros_like(db_ref)

        g_taps = _conv_taps(g_ref[...], jnp.where(i > 0, gh_ref[...], 0.0), 3)
        v_taps = _conv_taps(v_ref[...], jnp.where(i > 0, vh_ref[...], 0.0), 3)
        gate = _causal_conv(g_taps, wg_ref[...], bg_ref[...])
        val = _causal_conv(v_taps, wv_ref[...], bv_ref[...])
        gl, dgl = _gelu_and_grad(gate)
        dact = da_ref[...]
        is_gate = jj < FFN_NJ
        d = jnp.where(is_gate, dact * val * dgl, dact * gl)
        dup_ref[...] = d
        db_ref[...] += jnp.sum(d, axis=0, keepdims=True)
        dw_ref[...] += jnp.concatenate(
            [jnp.sum(d * jnp.where(is_gate, g_taps[2 - k], v_taps[2 - k]), axis=0, keepdims=True)
             for k in range(3)], axis=0)

    jm = lambda jj: jj % FFN_NJ
    blk = lambda off: pl.BlockSpec((tm, FFN_TC), lambda jj, i: (i, jm(jj) + off))
    halo = lambda off: pl.BlockSpec(
        (SUBLANES, FFN_TC), lambda jj, i: (jnp.maximum(i * (tm // SUBLANES) - 1, 0), jm(jj) + off))
    wsp = lambda rows, off: pl.BlockSpec((rows, FFN_TC), lambda jj, i: (0, jm(jj) + off))
    return pl.pallas_call(
        body, name="ffn_bwd_act", grid=(2 * FFN_NJ, t // tm),
        in_specs=[blk(0), halo(0), blk(FFN_NJ), halo(FFN_NJ), blk(0),
                  wsp(3, 0), wsp(3, FFN_NJ), wsp(1, 0), wsp(1, FFN_NJ)],
        out_specs=[pl.BlockSpec((tm, FFN_TC), lambda jj, i: (i, jj)),
                   pl.BlockSpec((3, FFN_TC), lambda jj, i: (0, jj)),
                   pl.BlockSpec((1, FFN_TC), lambda jj, i: (0, jj))],
        out_shape=[jax.ShapeDtypeStruct((t, 2 * D_FF), F32), jax.ShapeDtypeStruct((3, 2 * D_FF), F32),
                   jax.ShapeDtypeStruct((1, 2 * D_FF), F32)],
        compiler_params=_params(("parallel", "arbitrary")),
    )(up_pre, up_pre, up_pre, up_pre, dact, w_fconv, w_fconv, b_fconv, b_fconv)


def _ffn_bwd_conv(dup, w_fconv):
    t = dup.shape[0]
    tm = _ffn_tm(t)
    nt = t // tm

    def body(d_ref, dh_ref, w_ref, out_ref):
        d = d_ref[...]
        halo = jnp.where(pl.program_id(0) < nt - 1, dh_ref[...], 0.0)
        wv = w_ref[...]
        out = wv[2:3] * d + wv[1:2] * _shift_up(d, 1, halo) + wv[0:1] * _shift_up(d, 2, halo)
        out_ref[...] = out.astype(out_ref.dtype)

    return pl.pallas_call(
        body, name="ffn_bwd_conv", grid=(nt, 2 * FFN_NJ),
        in_specs=[pl.BlockSpec((tm, FFN_TC), lambda i, j: (i, j)),
                  pl.BlockSpec((SUBLANES, FFN_TC),
                               lambda i, j: (jnp.minimum((i + 1) * (tm // SUBLANES), t // SUBLANES - 1), j)),
                  pl.BlockSpec((3, FFN_TC), lambda i, j: (0, j))],
        out_specs=pl.BlockSpec((tm, FFN_TC), lambda i, j: (i, j)),
        out_shape=jax.ShapeDtypeStruct((t, 2 * D_FF), BF16),
        compiler_params=_params(("parallel", "parallel")),
    )(dup, dup, w_fconv)


def _xattn_probs(cq, ck, h):
    sl = slice(h * LANES, (h + 1) * LANES)
    s = _dot_nt(cq[:, sl], ck[:, sl]) * X_SCALE
    e = jnp.exp(s - jnp.max(s, axis=1, keepdims=True))
    return e / jnp.sum(e, axis=1, keepdims=True), sl


def _xattn_fwd_fn(cq, ck, cv):
    outs = []
    for h in range(X_HEADS):
        p, sl = _xattn_probs(cq, ck, h)
        outs.append(_dot(p, cv[:, sl]))
    return (jnp.concatenate(outs, axis=1),)


def _xattn_bwd_fn(cq, dco, ck, cv):
    dcq, dck, dcv = [], [], []
    for h in range(X_HEADS):
        p, sl = _xattn_probs(cq, ck, h)
        dcv.append(_dot_tn(p, dco[:, sl]))
        dp = _dot_nt(dco[:, sl], cv[:, sl])
        ds = p * (dp - jnp.sum(p * dp, axis=1, keepdims=True)) * X_SCALE
        dcq.append(_dot(ds, ck[:, sl]))
        dck.append(_dot_tn(ds, cq[:, sl]))
    return jnp.concatenate(dcq, axis=1), jnp.concatenate(dck, axis=1), jnp.concatenate(dcv, axis=1)


def _perm_w_in(w):
    q = w[:, :3072].reshape(D_MODEL, N_HEADS, 192)
    qn = q[:, :, :128].reshape(D_MODEL, 2048)
    qr = jnp.pad(q[:, :, 128:], ((0, 0), (0, 0), (0, 64))).reshape(D_MODEL, 2048)
    ckv = w[:, 3072:3584]
    kr = jnp.pad(w[:, 3584:3648], ((0, 0), (0, 64)))
    rest = w[:, 3648:]
    pad = jnp.zeros((D_MODEL, PROJ_W - 12928), w.dtype)
    return jnp.concatenate([qn, qr, rest, ckv, kr, pad], axis=1)


def _unperm_dw_in(dw):
    qn = dw[:, :2048].reshape(D_MODEL, N_HEADS, 128)
    qr = dw[:, 2048:4096].reshape(D_MODEL, N_HEADS, 128)[:, :, :64]
    q = jnp.concatenate([qn, qr], axis=2).reshape(D_MODEL, 3072)
    return jnp.concatenate([q, dw[:, 12288:12800], dw[:, 12800:12864], dw[:, 4096:12288]], axis=1)


def _local_step(x, mem, positions, tgt, wb, ws):
    t = x.shape[0]
    n_mem = mem.shape[0]
    tm = min(256, t)
    row = lambda v: v.reshape(1, -1)
    g_pre_mix, g_post_mix, g_ckv = row(ws["g_pre_mix"]), row(ws["g_post_mix"]), row(ws["g_ckv"])
    g_pre_x, g_post_x, g_mem = row(ws["g_pre_x"]), row(ws["g_post_x"]), row(ws["g_mem"])
    g_pre_ffn, g_post_ffn = row(ws["g_pre_ffn"]), row(ws["g_post_ffn"])
    b_conv, lam, b_fconv = row(ws["b_conv_lru"]), row(ws["lru_lambda"]), row(ws["b_fconv"])
    b_rg, b_ig = row(ws["b_rg"]), row(ws["b_ig"])
    w_rg, w_ig = ws["w_rg"].astype(BF16), ws["w_ig"].astype(BF16)
    w_in = _perm_w_in(wb["w_in"])
    w_conv = wb["w_conv_lru"].astype(F32)
    w_fconv = wb["w_fconv"].astype(F32)
    W = D_MODEL

    def dw(name, a, b, owners=1, tn=1024):
        g = _mm(name, a, b, "tn", out_dtype=BF16, owners=owners, tn=tn)
        return g.reshape(8, -1, g.shape[-1])

    cos_t, s1_t, s2_t = _rope_tables(positions.reshape(t, 1), tm)
    (h1,) = _rows("rms_pre_mix", lambda x, g: (_rms(x, g),), t, tm, [(x, W, 0)], [g_pre_mix], [(W, BF16)])
    proj = _mm("proj", h1, w_in, "nn")

    def rope_ckv_fn(qr, ckv, kr, c, s1, s2, g):
        return _rope_heads(qr, c, s1, s2), _rope_group(kr, c, s1, s2), _rms(ckv, g)

    qr, kr, ckvn = _rows("rope_ckv", rope_ckv_fn, t, tm,
                         [(proj, W, SEG_QR), (proj, 512, CKV_BLOCK), (proj, LANES, KR_BLOCK),
                          (cos_t, LANES, 0), (s1_t, LANES, 0), (s2_t, LANES, 0)], [g_ckv],
                         [(W, BF16), (LANES, BF16), (512, BF16)])
    kv = _mm("kv_up", ckvn, wb["w_ukv"], "nn", out_dtype=BF16)
    o, lse = _flash_fwd(proj, qr, kv, kr)
    o_mla = _mm("o_mla", o, wb["w_o_mla"], "nn")
    oli, hl = _lru_fwd(proj, w_conv, b_conv, w_rg, b_rg, w_ig, b_ig, lam)
    o_lru = _mm("o_lru", oli, wb["w_o_lru"], "nn")
    (merged,) = _rows("merge", lambda gm, gl, a, b: (_sigmoid(gm) * a + _sigmoid(gl) * b,), t, tm,
                      [(proj, W, SEG_GM), (proj, W, SEG_GL), (o_mla, W, 0), (o_lru, W, 0)], [], [(W, BF16)])
    z1 = _mm("w_out", merged, wb["w_out"], "nn")

    def post_fn(x, z, g_post, g_pre):
        x1 = x + _rms(z, g_post)
        return x1, _rms(x1, g_pre)

    x1, h2 = _rows("post_mix", post_fn, t, tm, [(x, W, 0), (z1, W, 0)], [g_post_mix, g_pre_x],
                   [(W, F32), (W, BF16)])
    cq = _mm("cq", h2, wb["w_cq"], "nn", out_dtype=BF16)
    (mn,) = _rows("rms_mem", lambda m, g: (_rms(m, g),), n_mem, n_mem, [(mem, W, 0)], [g_mem], [(W, BF16)])
    ck = _mm("ck", mn, wb["w_ck"], "nn", out_dtype=BF16)
    cv = _mm("cv", mn, wb["w_cv"], "nn", out_dtype=BF16)
    (co,) = _rows("xattn_fwd", _xattn_fwd_fn, t, tm, [(cq, 512, 0)], [ck, cv], [(512, BF16)])
    z2 = _mm("w_co", co, wb["w_co"], "nn")
    x2, h3 = _rows("post_x", post_fn, t, tm, [(x1, W, 0), (z2, W, 0)], [g_post_x, g_pre_ffn],
                   [(W, F32), (W, BF16)])
    up_pre = _mm("w_up", h3, wb["w_up"], "nn")
    act = _ffn_fwd(up_pre, w_fconv, b_fconv)
    z3 = _mm("w_down", act, wb["w_down"], "nn")

    def loss_fn(x2, z3, tgt, g):
        err = x2 + _rms(z3, g) - tgt
        dy = err * (1.0 / W)
        dz, dg = _rms_bwd(z3, g, dy)
        part = 0.5 * jnp.sum(err * err) * (1.0 / W)
        return dy, dz, jnp.zeros((SUBLANES, LANES), F32) + part, dg

    dy, dz3, loss_acc, dg_post_ffn = _rows(
        "loss", loss_fn, t, tm, [(x2, W, 0), (z3, W, 0), (tgt, W, 0)], [g_post_ffn],
        [(W, F32), (W, BF16)], [(SUBLANES, LANES), (1, W)])
    grads = {"g_post_ffn": dg_post_ffn}
    dact = _mm("d_act", dz3, wb["w_down"], "nt")
    grads["w_down"] = dw("dw_down", act, dz3)
    dup, grads["w_fconv"], grads["b_fconv"] = _ffn_bwd_act(up_pre, dact, w_fconv, b_fconv)
    dup_pre = _ffn_bwd_conv(dup, w_fconv)
    dh3 = _mm("d_h3", dup_pre, wb["w_up"], "nt")
    grads["w_up"] = dw("dw_up", h3, dup_pre, owners=4, tn=1408)

    def res_bwd_fn(dres, xa, dh, z, g_pre, g_post):
        dxa, dg_pre = _rms_bwd(xa, g_pre, dh)
        dxa = dres + dxa
        dz, dg_post = _rms_bwd(z, g_post, dxa)
        return dxa, dz, dg_pre, dg_post

    dx2, dz2, grads["g_pre_ffn"], grads["g_post_x"] = _rows(
        "bwd_post_x", res_bwd_fn, t, tm, [(dy, W, 0), (x2, W, 0), (dh3, W, 0), (z2, W, 0)],
        [g_pre_ffn, g_post_x], [(W, F32), (W, BF16)], [(1, W), (1, W)])
    dco = _mm("d_co", dz2, wb["w_co"], "nt")
    grads["w_co"] = dw("dw_co", co, dz2, owners=4)
    dcq, dck, dcv = _rows("xattn_bwd", _xattn_bwd_fn, t, tm, [(cq, 512, 0), (dco, 512, 0)], [ck, cv],
                          [(512, BF16)], [(n_mem, 512), (n_mem, 512)])
    dh2 = _mm("d_h2", dcq, wb["w_cq"], "nt")
    grads["w_cq"] = dw("dw_cq", h2, dcq)
    grads["w_ck"] = dw("dw_ck", mn, dck)
    grads["w_cv"] = dw("dw_cv", mn, dcv)
    dmn_k = _mm("d_mn_k", dck, wb["w_ck"], "nt")
    dmn_v = _mm("d_mn_v", dcv, wb["w_cv"], "nt")
    (grads["g_mem"],) = _rows("dg_mem", lambda m, a, b: (jnp.sum((a + b) * m * _rsq(m), axis=0, keepdims=True),),
                              n_mem, n_mem, [(mem, W, 0), (dmn_k, W, 0), (dmn_v, W, 0)], [], [], [(1, W)])
    dx1, dz1, grads["g_pre_x"], grads["g_post_mix"] = _rows(
        "bwd_post_mix", res_bwd_fn, t, tm, [(dx2, W, 0), (x1, W, 0), (dh2, W, 0), (z1, W, 0)],
        [g_pre_x, g_post_mix], [(W, F32), (W, BF16)], [(1, W), (1, W)])
    dmerged = _mm("d_merged", dz1, wb["w_out"], "nt")
    grads["w_out"] = dw("dw_out", merged, dz1)

    def merge_bwd_fn(dm, gm, gl, a, b):
        sm, sl = _sigmoid(gm), _sigmoid(gl)
        return dm * sm, dm * sl, dm * a * sm * (1.0 - sm), dm * b * sl * (1.0 - sl)

    do_mla, do_lru, dgm, dgl = _rows(
        "merge_bwd", merge_bwd_fn, t, tm,
        [(dmerged, W, 0), (proj, W, SEG_GM), (proj, W, SEG_GL), (o_mla, W, 0), (o_lru, W, 0)], [],
        [(W, BF16)] * 4)
    do = _mm("d_o", do_mla, wb["w_o_mla"], "nt")
    grads["w_o_mla"] = dw("dw_o_mla", o, do_mla)
    dqn, dqr_pre = _flash_dq(proj, qr, kv, kr, do, o, lse)
    dkv, dkr_h = _flash_dkv(proj, qr, kv, kr, do, o, lse)
    dckvn = _mm("d_ckvn", dkv, wb["w_ukv"], "nt")
    grads["w_ukv"] = dw("dw_ukv", ckvn, dkv, owners=4)

    def rope_bwd_fn(dqr, dckvn, ckv, c, s1, s2, dkr_h, g):
        dkr = dkr_h[0]
        for h in range(1, N_HEADS):
            dkr = dkr + dkr_h[h]
        dckv, dg = _rms_bwd(ckv, g, dckvn)
        return _rope_heads(dqr, c, -s1, -s2), _rope_group(dkr, c, -s1, -s2), dckv, dg

    dqr, dkr, dckv, grads["g_ckv"] = _rope_bwd(rope_bwd_fn, t, tm, dqr_pre, dckvn, proj, cos_t, s1_t, s2_t, dkr_h, g_ckv)
    doli = _mm("d_oli", do_lru, wb["w_o_lru"], "nt")
    grads["w_o_lru"] = dw("dw_o_lru", oli, do_lru)
    (dlx, dly, grads["w_conv_lru"], grads["b_conv_lru"], grads["w_rg"], grads["b_rg"], grads["w_ig"],
     grads["b_ig"], grads["lru_lambda"]) = _lru_bwd(proj, hl, doli, w_conv, b_conv, w_rg, b_rg, w_ig, b_ig, lam)
    dproj = jnp.concatenate([dqn, dqr, dlx, dly, dgm, dgl, dckv, dkr,
                             jnp.zeros((t, PROJ_W - 12928), BF16)], axis=1)
    dh1 = _mm("d_h1", dproj, w_in, "nt")
    dw_in = _unperm_dw_in(_mm("dw_in", h1, dproj, "tn", out_dtype=BF16))
    grads["w_in"] = dw_in.reshape(W, 4, -1).transpose(1, 0, 2).reshape(8, W // 2, -1)

    def in_bwd_fn(dres, x, dh, g):
        dx, dg = _rms_bwd(x, g, dh)
        return dres + dx, dg

    grad_x, grads["g_pre_mix"] = _rows("bwd_pre_mix", in_bwd_fn, t, tm, [(dx1, W, 0), (x, W, 0), (dh1, W, 0)],
                                       [g_pre_mix], [(W, F32)], [(1, W)])
    return loss_acc[0, 0], grad_x, grads


def _rope_bwd(fn, t, tm, dqr_pre, dckvn, proj, cos_t, s1_t, s2_t, dkr_h, g_ckv):
    W = D_MODEL

    def body(dqr_ref, dck_ref, ckv_ref, c_ref, s1_ref, s2_ref, dkrh_ref, g_ref, o1, o2, o3, o4):
        r1, r2, r3, dg = fn(dqr_ref[...], dck_ref[...], ckv_ref[...], c_ref[...], s1_ref[...], s2_ref[...],
                            dkrh_ref[...], g_ref[...])
        o1[...] = r1.astype(o1.dtype)
        o2[...] = r2.astype(o2.dtype)
        o3[...] = r3.astype(o3.dtype)

        @pl.when(pl.program_id(0) == 0)
        def _():
            o4[...] = jnp.zeros_like(o4)

        o4[...] += dg

    rb = lambda w, cb=0: pl.BlockSpec((tm, w), lambda i, cb=cb: (i, cb))
    return pl.pallas_call(
        body, name="rope_bwd", grid=(t // tm,),
        in_specs=[rb(W), rb(512), rb(512, CKV_BLOCK), rb(LANES), rb(LANES), rb(LANES),
                  pl.BlockSpec((N_HEADS, tm, LANES), lambda i: (0, i, 0)),
                  pl.BlockSpec((1, 512), lambda i: (0, 0))],
        out_specs=[rb(W), rb(LANES), rb(512), pl.BlockSpec((1, 512), lambda i: (0, 0))],
        out_shape=[jax.ShapeDtypeStruct((t, W), BF16), jax.ShapeDtypeStruct((t, LANES), BF16),
                   jax.ShapeDtypeStruct((t, 512), BF16), jax.ShapeDtypeStruct((1, 512), F32)],
        compiler_params=_params(("arbitrary",)),
    )(dqr_pre, dckvn, proj, cos_t, s1_t, s2_t, dkr_h, g_ckv)


HBM = pl.BlockSpec(memory_space=pl.ANY)


def _me():
    return lax.axis_index("x"), lax.axis_index("y"), lax.axis_index("c")


def _gather_chips(arrs):
    n = len(arrs)

    def body(*refs):
        srcs, outs = refs[:n], refs[n:2 * n]
        send_sems, recv_sems, local_sems = refs[2 * n:]
        x, y, c = _me()
        chips = [(1 - x, y), (x, 1 - y), (1 - x, 1 - y)]

        def copy(i, k, slot, to):
            return pltpu.make_async_remote_copy(
                src_ref=srcs[i], dst_ref=outs[i].at[slot], send_sem=send_sems.at[i, k], recv_sem=recv_sems.at[i, k],
                device_id=to, device_id_type=MESH)

        local = [pltpu.make_async_copy(srcs[i], outs[i].at[2 * x + y], local_sems.at[i]) for i in range(n)]
        sends = [copy(i, k, 2 * x + y, (px, py, c)) for i in range(n) for k, (px, py) in enumerate(chips)]
        for cp in local + sends:
            cp.start()
        for i in range(n):
            for k, (px, py) in enumerate(chips):
                copy(i, k, 2 * px + py, (px, py, c)).wait_recv()
        for cp in sends:
            cp.wait_send()
        for cp in local:
            cp.wait()

    return pl.pallas_call(
        body, name="gather_weights", in_specs=[HBM] * n, out_specs=[HBM] * n,
        out_shape=[jax.ShapeDtypeStruct((4,) + a.shape, a.dtype) for a in arrs],
        scratch_shapes=[pltpu.SemaphoreType.DMA((n, 3)), pltpu.SemaphoreType.DMA((n, 3)), pltpu.SemaphoreType.DMA((n,))],
    )(*arrs)


def _exchange_all(name, arrs, scatter):
    n = len(arrs)

    def body(*refs):
        srcs, outs = refs[:n], refs[n:2 * n]
        send_sems, recv_sems, local_sems = refs[2 * n:]
        x, y, c = _me()
        me = 4 * x + 2 * y + c

        def copy(i, q):
            return pltpu.make_async_remote_copy(
                src_ref=srcs[i].at[q] if scatter else srcs[i], dst_ref=outs[i].at[me],
                send_sem=send_sems.at[i, q], recv_sem=recv_sems.at[i, me],
                device_id=(q // 4, (q // 2) % 2, q % 2), device_id_type=MESH)

        def arrival(i, s):
            return pltpu.make_async_remote_copy(
                src_ref=srcs[i].at[s] if scatter else srcs[i], dst_ref=outs[i].at[s],
                send_sem=send_sems.at[i, s], recv_sem=recv_sems.at[i, s],
                device_id=(s // 4, (s // 2) % 2, s % 2), device_id_type=MESH)

        local = [pltpu.make_async_copy(srcs[i].at[me] if scatter else srcs[i], outs[i].at[me], local_sems.at[i])
                 for i in range(n)]
        for cp in local:
            cp.start()
        for q in range(8):
            @pl.when(me != q)
            def _(q=q):
                for i in range(n):
                    copy(i, q).start()
        for s in range(8):
            @pl.when(me != s)
            def _(s=s):
                for i in range(n):
                    arrival(i, s).wait_recv()
        for q in range(8):
            @pl.when(me != q)
            def _(q=q):
                for i in range(n):
                    copy(i, q).wait_send()
        for cp in local:
            cp.wait()

    return pl.pallas_call(
        body, name=name, in_specs=[HBM] * n, out_specs=[HBM] * n,
        out_shape=[jax.ShapeDtypeStruct((8,) + (a.shape[1:] if scatter else a.shape), a.dtype) for a in arrs],
        scratch_shapes=[pltpu.SemaphoreType.DMA((n, 8)), pltpu.SemaphoreType.DMA((n, 8)), pltpu.SemaphoreType.DMA((n,))],
    )(*arrs)


D2D_CHUNK_BYTES = 2 * 1024 * 1024


def _gather_cores(arrs):
    n = len(arrs)
    chunks = []
    for i, a in enumerate(arrs):
        r = a.shape[0]
        k = 1
        while a.size * a.dtype.itemsize // k > D2D_CHUNK_BYTES and r % (2 * k) == 0 and (r // (2 * k)) % SUBLANES == 0:
            k *= 2
        chunks += [(i, j * (r // k), r // k) for j in range(k)]
    nc = len(chunks)

    def body(*refs):
        srcs, outs = refs[:n], refs[n:2 * n]
        send_sems, recv_sems, local_sems = refs[2 * n:]
        x, y, c = _me()

        def copy(j, slot):
            i, r0, nr = chunks[j]
            return pltpu.make_async_remote_copy(
                src_ref=srcs[i].at[pl.ds(r0, nr)], dst_ref=outs[i].at[slot, pl.ds(r0, nr)],
                send_sem=send_sems.at[j], recv_sem=recv_sems.at[j],
                device_id=(x, y, 1 - c), device_id_type=MESH)

        local = [pltpu.make_async_copy(srcs[i].at[pl.ds(r0, nr)], outs[i].at[c, pl.ds(r0, nr)], local_sems.at[j])
                 for j, (i, r0, nr) in enumerate(chunks)]
        sends = [copy(j, c) for j in range(nc)]
        for cp in sends:
            cp.start()
        for cp in local:
            cp.start()
        for j in range(nc):
            copy(j, 1 - c).wait_recv()
        for cp in sends:
            cp.wait_send()
        for cp in local:
            cp.wait()

    return pl.pallas_call(
        body, name="gather_cores", in_specs=[HBM] * n, out_specs=[HBM] * n,
        out_shape=[jax.ShapeDtypeStruct((2,) + a.shape, a.dtype) for a in arrs],
        scratch_shapes=[pltpu.SemaphoreType.DMA((nc,))] * 3,
    )(*arrs)


def _row_tile(rows, row_bytes, budget):
    for t in (2048, 1024, 512, 256, 128, 64, 32, 16, 8):
        if rows % t == 0 and t * row_bytes <= budget:
            return t
    return rows


def _sum_slots(name, buf):
    _, r, w = buf.shape
    tr = _row_tile(r, 8 * w * buf.dtype.itemsize, 8 * 1024 * 1024)

    def body(b_ref, o_ref):
        acc = b_ref[0].astype(F32)
        for s in range(1, 8):
            acc = acc + b_ref[s].astype(F32)
        o_ref[...] = acc

    return pl.pallas_call(
        body, name=name, grid=(r // tr,),
        in_specs=[pl.BlockSpec((8, tr, w), lambda i: (0, i, 0))],
        out_specs=pl.BlockSpec((tr, w), lambda i: (i, 0)),
        out_shape=jax.ShapeDtypeStruct((r, w), F32),
        compiler_params=_params(("parallel",)),
    )(buf)


def _adamw(name, w, g, m, v, tr):
    def fn(w, g, m, v):
        m = ADAM_B1 * m + (1.0 - ADAM_B1) * g
        v = ADAM_B2 * v + (1.0 - ADAM_B2) * (g * g)
        m_hat = m / (1.0 - ADAM_B1 ** ADAM_STEP)
        v_hat = v / (1.0 - ADAM_B2 ** ADAM_STEP)
        delta = -ADAM_LR * (m_hat / (jnp.sqrt(v_hat) + ADAM_EPS) + ADAM_WD * w)
        return delta, m, v

    r, c = w.shape
    return _rows(name, fn, r, tr, [(a, c, 0) for a in (w, g, m, v)], [], [(c, F32)] * 3)


def _pack(arrays, n_rows, dtype):
    flat = jnp.concatenate([a.reshape(-1).astype(dtype) for a in arrays])
    return jnp.pad(flat, (0, n_rows * FLAT_W - flat.shape[0])).reshape(n_rows, FLAT_W)


def _unpack(flat, shapes):
    flat = flat.reshape(-1)
    out, off = [], 0
    for s in shapes:
        n = math.prod(s)
        out.append(flat[off:off + n].reshape(s))
        off += n
    return out


def kernel(x, mem, positions, g_pre_mix, g_post_mix, w_in, g_ckv, w_ukv, w_o_mla, w_conv_lru, b_conv_lru, w_rg, b_rg, w_ig, b_ig, lru_lambda, w_o_lru, w_out, g_pre_x, g_post_x, g_mem, w_cq, w_ck, w_cv, w_co, g_pre_ffn, g_post_ffn, w_up, w_fconv, b_fconv, w_down, loss_target, m_g_pre_mix, m_g_post_mix, m_w_in, m_g_ckv, m_w_ukv, m_w_o_mla, m_w_conv_lru, m_b_conv_lru, m_w_rg, m_b_rg, m_w_ig, m_b_ig, m_lru_lambda, m_w_o_lru, m_w_out, m_g_pre_x, m_g_post_x, m_g_mem, m_w_cq, m_w_ck, m_w_cv, m_w_co, m_g_pre_ffn, m_g_post_ffn, m_w_up, m_w_fconv, m_b_fconv, m_w_down, v_g_pre_mix, v_g_post_mix, v_w_in, v_g_ckv, v_w_ukv, v_w_o_mla, v_w_conv_lru, v_b_conv_lru, v_w_rg, v_b_rg, v_w_ig, v_b_ig, v_lru_lambda, v_w_o_lru, v_w_out, v_g_pre_x, v_g_post_x, v_g_mem, v_w_cq, v_w_ck, v_w_cv, v_w_co, v_g_pre_ffn, v_g_post_ffn, v_w_up, v_w_fconv, v_b_fconv, v_w_down):
    given = dict(locals())
    small_names = [n for n, _ in SMALL]
    conv_names = [n for n, _, _ in CONV]

    mine = [given[n][0].astype(BF16) for n, _, _ in BIG] + [given[n][0] for n in conv_names]
    gathered = _gather_chips(mine)
    wb = {}
    for (n, _, ax), g in zip(BIG + CONV, gathered):
        wb[n] = g.reshape(-1, g.shape[-1]) if ax == 0 else jnp.concatenate([g[j] for j in range(4)], axis=1)
    ws = {n: given[n][0] for n in small_names}

    loss_part, grad_x, grads = _local_step(x[0], mem[0], positions[0], loss_target[0], wb, ws)
    loss = lax.psum(loss_part, ("x", "y", "c"))

    landed = _exchange_all("scatter_grads", [grads[n] for n, _, _ in BIG], scatter=True)
    pieces = [_sum_slots("reduce_" + n, buf) for (n, _, _), buf in zip(BIG, landed)]
    g_big = {n: g.reshape(s) for (n, s, _), g in zip(BIG, _gather_cores(pieces))}

    xs_shapes = [s for _, s in SMALL] + [(s[0], 4 * s[1]) for _, s, _ in CONV]
    s_part = _pack([grads[n].reshape(s) for n, s in zip(small_names + conv_names, xs_shapes)], SMALL_XCHG_ROWS, F32)
    (s_all,) = _exchange_all("gather_small", [s_part], scatter=False)
    g_small = dict(zip(small_names + conv_names, _unpack(_sum_slots("reduce_small", s_all), xs_shapes)))
    chip = 2 * lax.axis_index("x") + lax.axis_index("y")
    for n, s, _ in CONV:
        g_small[n] = lax.dynamic_slice_in_dim(g_small[n], chip * s[1], s[1], axis=1)

    out = {n: (g_big[n],) + tuple(_adamw("adamw_" + n, given[n][0], g_big[n], given["m_" + n][0], given["v_" + n][0],
                                         _row_tile(s[0], 4 * s[1], 1536 * 1024))) for n, s, _ in BIG}
    sm_shapes = [s for _, s in SMALL] + [s for _, s, _ in CONV]
    pack_small = lambda pre: _pack([given[pre + n][0] for n in small_names + conv_names], SMALL_ROWS, F32)
    g_flat = _pack([g_small[n] for n in small_names + conv_names], SMALL_ROWS, F32)
    small_out = (g_flat,) + tuple(_adamw("adamw_small", pack_small(""), g_flat, pack_small("m_"), pack_small("v_"), 280))
    for kind in range(4):
        for n, a in zip(small_names + conv_names, _unpack(small_out[kind], sm_shapes)):
            out.setdefault(n, [None] * 4)
            out[n] = list(out[n])
            out[n][kind] = a
    outs = [out[n][kind][None] for kind in range(4) for n in WEIGHTS]
    return (loss, grad_x[None], *outs)
```

```python
import functools
import math

import jax
import jax.numpy as jnp
from jax import lax
from jax.experimental import pallas as pl
from jax.experimental.pallas import tpu as pltpu

F32 = jnp.float32
BF16 = jnp.bfloat16
MESH = pl.DeviceIdType.MESH

EPS = 1e-6
D_MODEL = 2048
N_HEADS = 16
HEAD_W = 128
ROPE_HALF = 32
SOFTMAX_SCALE = 192 ** -0.5
ROPE_THETA = 10000.0
LRU_BLOCKS = 16
LRU_C = 8.0
X_HEADS = 4
X_SCALE = 128 ** -0.5
D_FF = 5632

ADAM_LR = 0.001
ADAM_B1 = 0.9
ADAM_B2 = 0.999
ADAM_EPS = 1e-08
ADAM_WD = 0.01
ADAM_STEP = 10

V7X_VMEM_LIMIT = 56 * 1024 * 1024
LANES = 128
SUBLANES = 8
ATTN_TILE = 1024
ATTN_CHUNK = 256
MM_BLOCK_BYTES = 4 * 1024 * 1024

PROJ_W = 13312
SEG_QN, SEG_QR, SEG_LX, SEG_LY, SEG_GM, SEG_GL = range(6)
CKV_BLOCK = 24
KR_BLOCK = 100

BIG = (
    ("w_in", (2048, 2960), 1), ("w_ukv", (512, 1024), 1), ("w_o_mla", (512, 2048), 0),
    ("w_o_lru", (512, 2048), 0), ("w_out", (512, 2048), 0),
    ("w_cq", (512, 512), 0), ("w_ck", (512, 512), 0), ("w_cv", (512, 512), 0),
    ("w_co", (512, 512), 1), ("w_up", (2048, 2816), 1), ("w_down", (1408, 2048), 0),
)
CONV = (("w_conv_lru", (4, 512), 1), ("w_fconv", (3, 2816), 1))
SMALL = (
    ("g_pre_mix", (2048,)), ("g_post_mix", (2048,)), ("g_ckv", (512,)), ("b_conv_lru", (2048,)),
    ("w_rg", (16, 128, 128)), ("b_rg", (16, 128)), ("w_ig", (16, 128, 128)), ("b_ig", (16, 128)),
    ("lru_lambda", (2048,)), ("g_pre_x", (2048,)), ("g_post_x", (2048,)), ("g_mem", (2048,)),
    ("g_pre_ffn", (2048,)), ("g_post_ffn", (2048,)), ("b_fconv", (11264,)),
)
WEIGHTS = ("g_pre_mix", "g_post_mix", "w_in", "g_ckv", "w_ukv", "w_o_mla", "w_conv_lru", "b_conv_lru",
           "w_rg", "b_rg", "w_ig", "b_ig", "lru_lambda", "w_o_lru", "w_out", "g_pre_x", "g_post_x", "g_mem",
           "w_cq", "w_ck", "w_cv", "w_co", "g_pre_ffn", "g_post_ffn", "w_up", "w_fconv", "b_fconv", "w_down")
FLAT_W = 1024
SMALL_XCHG_ROWS = 640
SMALL_ROWS = 560


def _params(sem=None):
    return pltpu.CompilerParams(dimension_semantics=sem, vmem_limit_bytes=V7X_VMEM_LIMIT)


def _tile(dim, pref):
    if dim <= pref:
        return dim
    for t in range(pref - pref % LANES, 0, -LANES):
        if dim % t == 0:
            return t
    raise ValueError((dim, pref))


def _mm(name, a, b, mode, out_dtype=F32, tm=1024, tn=1024, tk=None, owners=1):
    if mode == "nn":
        (m, k), (_, n) = a.shape, b.shape
    elif mode == "nt":
        (m, k), (n, _) = a.shape, b.shape
    else:
        (k, m), (_, n) = a.shape, b.shape
    tm, tn = _tile(m, tm), _tile(n // owners, tn)
    if tk is None:
        tk = MM_BLOCK_BYTES // max(tm * a.dtype.itemsize, tn * b.dtype.itemsize)
    tk = _tile(k, tk)
    nk = k // tk
    per_owner = n // owners // tn
    if owners == 1:
        out_spec = pl.BlockSpec((tm, tn), lambda i, j, kk: (i, j))
        out_shape = jax.ShapeDtypeStruct((m, n), out_dtype)
    else:
        out_spec = pl.BlockSpec((None, tm, tn), lambda i, j, kk: (j // per_owner, i, j % per_owner))
        out_shape = jax.ShapeDtypeStruct((owners, m, n // owners), out_dtype)
    if mode == "nn":
        a_spec = pl.BlockSpec((tm, tk), lambda i, j, kk: (i, kk))
        b_spec = pl.BlockSpec((tk, tn), lambda i, j, kk: (kk, j))
        dims = (((1,), (0,)), ((), ()))
    elif mode == "nt":
        a_spec = pl.BlockSpec((tm, tk), lambda i, j, kk: (i, kk))
        b_spec = pl.BlockSpec((tn, tk), lambda i, j, kk: (j, kk))
        dims = (((1,), (1,)), ((), ()))
    else:
        a_spec = pl.BlockSpec((tk, tm), lambda i, j, kk: (kk, i))
        b_spec = pl.BlockSpec((tk, tn), lambda i, j, kk: (kk, j))
        dims = (((0,), (0,)), ((), ()))

    def body(a_ref, b_ref, o_ref, acc_ref):
        kk = pl.program_id(2)

        @pl.when(kk == 0)
        def _():
            acc_ref[...] = jnp.zeros_like(acc_ref)

        acc_ref[...] += lax.dot_general(a_ref[...].astype(BF16), b_ref[...].astype(BF16), dims,
                                        preferred_element_type=F32)

        @pl.when(kk == nk - 1)
        def _():
            o_ref[...] = acc_ref[...].astype(o_ref.dtype)

    return pl.pallas_call(
        body, name=name, grid=(m // tm, n // tn, nk),
        in_specs=[a_spec, b_spec], out_specs=out_spec, out_shape=out_shape,
        scratch_shapes=[pltpu.VMEM((tm, tn), F32)],
        compiler_params=_params(("parallel", "parallel", "arbitrary")),
    )(a, b)


def _rows(name, fn, n_rows, tm, row_ins, full_ins, row_outs, acc_outs=()):
    in_specs, args = [], []
    for arr, width, cb in row_ins:
        in_specs.append(pl.BlockSpec((tm, width), lambda i, cb=cb: (i, cb)))
        args.append(arr)
    for arr in full_ins:
        in_specs.append(pl.BlockSpec(arr.shape, lambda i, nd=arr.ndim: (0,) * nd))
        args.append(arr)
    out_shape = [jax.ShapeDtypeStruct((n_rows, w), dt) for w, dt in row_outs]
    out_specs = [pl.BlockSpec((tm, w), lambda i: (i, 0)) for w, _ in row_outs]
    for s in acc_outs:
        out_shape.append(jax.ShapeDtypeStruct(s, F32))
        out_specs.append(pl.BlockSpec(s, lambda i, nd=len(s): (0,) * nd))
    n_in, n_ro = len(args), len(row_outs)

    def body(*refs):
        res = fn(*[r[...] for r in refs[:n_in]])
        outs = refs[n_in:]
        for r, v in zip(outs[:n_ro], res[:n_ro]):
            r[...] = v.astype(r.dtype)
        first = pl.program_id(0) == 0
        for r, v in zip(outs[n_ro:], res[n_ro:]):
            @pl.when(first)
            def _(r=r):
                r[...] = jnp.zeros_like(r)

            r[...] += v

    return pl.pallas_call(
        body, name=name, grid=(n_rows // tm,), in_specs=in_specs, out_specs=out_specs, out_shape=out_shape,
        compiler_params=_params(("arbitrary",)),
    )(*args)


def _rsq(x):
    return lax.rsqrt(jnp.mean(x * x, axis=-1, keepdims=True) + EPS)


def _rms(x, g):
    return x * _rsq(x) * g


def _rms_bwd(x, g, dy):
    r = _rsq(x)
    xh = x * r
    dxh = dy * g
    dx = r * (dxh - xh * jnp.mean(dxh * xh, axis=-1, keepdims=True))
    return dx, jnp.sum(dy * xh, axis=0, keepdims=True)


def _sigmoid(x):
    return 1.0 / (1.0 + jnp.exp(-x))


_GELU_C = math.sqrt(2.0 / math.pi)


def _gelu(x):
    return 0.5 * x * (1.0 + jnp.tanh(_GELU_C * (x + 0.044715 * x * x * x)))


def _gelu_and_grad(x):
    th = jnp.tanh(_GELU_C * (x + 0.044715 * x * x * x))
    g = 0.5 * x * (1.0 + th)
    dg = 0.5 * (1.0 + th) + 0.5 * x * (1.0 - th * th) * _GELU_C * (1.0 + 3.0 * 0.044715 * x * x)
    return g, dg


def _dot(a, b):
    return jnp.dot(a.astype(BF16), b.astype(BF16), preferred_element_type=F32)


def _dot_nt(a, b):
    return lax.dot_general(a.astype(BF16), b.astype(BF16), (((1,), (1,)), ((), ())), preferred_element_type=F32)


def _dot_tn(a, b):
    return lax.dot_general(a.astype(BF16), b.astype(BF16), (((0,), (0,)), ((), ())), preferred_element_type=F32)


def _row_iota(shape):
    return lax.broadcasted_iota(jnp.int32, shape, 0)


def _shift_down(x, k, halo):
    xs = pltpu.roll(x, k, 0)
    hs = pltpu.roll(halo, k, 0)
    first = jnp.where(_row_iota(hs.shape) < k, hs, xs[:SUBLANES])
    return jnp.concatenate([first, xs[SUBLANES:]], axis=0)


def _shift_up(x, k, halo):
    tm = x.shape[0]
    xs = pltpu.roll(x, tm - k, 0)
    hs = pltpu.roll(halo, SUBLANES - k, 0)
    last = jnp.where(_row_iota(hs.shape) >= SUBLANES - k, hs, xs[tm - SUBLANES:])
    return jnp.concatenate([xs[:tm - SUBLANES], last], axis=0)


def _conv_taps(x, halo, n_taps):
    return [x] + [_shift_down(x, k, halo) for k in range(1, n_taps)]


def _causal_conv(taps, w, b):
    kw = len(taps)
    y = b + w[kw - 1:kw] * taps[0]
    for s in range(1, kw):
        y = y + w[kw - 1 - s:kw - s] * taps[s]
    return y


def _rope_group(g, c, s1, s2):
    return g * c + pltpu.roll(g, LANES - ROPE_HALF, 1) * s1 + pltpu.roll(g, ROPE_HALF, 1) * s2


def _rope_heads(q, c, s1, s2):
    return jnp.concatenate(
        [_rope_group(q[:, h * HEAD_W:(h + 1) * HEAD_W], c, s1, s2) for h in range(N_HEADS)], axis=1)


def _rope_tables(pos_col, tm):
    inv_freq = ROPE_THETA ** (-jnp.arange(0, 2 * ROPE_HALF, 2, dtype=F32) / (2 * ROPE_HALF))
    invf = jnp.concatenate([inv_freq, inv_freq, jnp.zeros((LANES - 2 * ROPE_HALF,), F32)])[None, :]

    def fn(pos, invf):
        ang = pos.astype(F32) * invf
        c, s = jnp.cos(ang), jnp.sin(ang)
        lane = lax.broadcasted_iota(jnp.int32, ang.shape, 1)
        s1 = jnp.where(lane < ROPE_HALF, -s, 0.0)
        s2 = jnp.where((lane >= ROPE_HALF) & (lane < 2 * ROPE_HALF), s, 0.0)
        return c, s1, s2

    n = pos_col.shape[0]
    return _rows("rope_tables", fn, n, tm, [(pos_col, 1, 0)], [invf], [(LANES, F32)] * 3)


def _attn_tiles(t):
    tq = min(ATTN_TILE, t)
    nq = t // tq
    assert nq == 1 or nq % 2 == 0
    return tq, nq, max(nq // 2, 1), (nq + 1 if nq > 1 else 1)


def _row_chunks(tq):
    c = min(ATTN_CHUNK, tq)
    return [(r0, c) for r0 in range(0, tq, c)]


def _tri_q(p, s, nq):
    first = s <= p
    return jnp.where(first, p, nq - 1 - p), jnp.where(first, s, s - p - 1)


def _tri_k(p, s, nq):
    first = s < nq - p
    return jnp.where(first, p + s, s - 1), jnp.where(first, p, nq - 1 - p)


def _qk(qn_ref, qr_ref, kv_ref, kr_ref):
    q = jnp.concatenate([qn_ref[...].astype(BF16), qr_ref[...]], axis=1)
    k = jnp.concatenate([kv_ref[:, :HEAD_W], kr_ref[...]], axis=1)
    return q, k


def _scores(q, k, r0, diag):
    s = _dot_nt(q, k) * SOFTMAX_SCALE
    if diag:
        row = r0 + lax.broadcasted_iota(jnp.int32, s.shape, 0)
        col = lax.broadcasted_iota(jnp.int32, s.shape, 1)
        s = jnp.where(col <= row, s, -1e30)
    return s


def _flash_fwd(proj, qr, kv, kr):
    t = proj.shape[0]
    tq, nq, n_pairs, n_steps = _attn_tiles(t)

    def body(qn_ref, qr_ref, kv_ref, kr_ref, o_ref, lse_ref, m_s, l_s, acc_s):
        qi, ki = _tri_q(pl.program_id(1), pl.program_id(2), nq)

        @pl.when(ki == 0)
        def _():
            m_s[...] = jnp.full_like(m_s, -1e30)
            l_s[...] = jnp.zeros_like(l_s)
            acc_s[...] = jnp.zeros_like(acc_s)

        def step(diag):
            q, k = _qk(qn_ref, qr_ref, kv_ref, kr_ref)
            for r0, c in _row_chunks(tq):
                rows = slice(r0, r0 + c)
                n = r0 + c if diag else tq
                s = _scores(q[rows], k[:n], r0, diag)
                m_prev = m_s[rows]
                m_new = jnp.maximum(m_prev, jnp.max(s, axis=1, keepdims=True))
                alpha = jnp.exp(m_prev - m_new)
                p = jnp.exp(s - m_new)
                l_s[rows] = alpha * l_s[rows] + jnp.sum(p, axis=1, keepdims=True)
                acc_s[rows] = alpha * acc_s[rows] + _dot(p, kv_ref[:n, HEAD_W:])
                m_s[rows] = m_new

        @pl.when(ki < qi)
        def _():
            step(False)

        @pl.when(ki == qi)
        def _():
            step(True)
            o_ref[...] = acc_s[...] / l_s[...]
            lse_ref[...] = m_s[...] + jnp.log(l_s[...])

    qb = lambda p, s: _tri_q(p, s, nq)[0]
    kb = lambda p, s: _tri_q(p, s, nq)[1]
    return pl.pallas_call(
        body, name="mla_fwd", grid=(N_HEADS, n_pairs, n_steps),
        in_specs=[pl.BlockSpec((tq, HEAD_W), lambda h, p, s: (qb(p, s), h)),
                  pl.BlockSpec((tq, HEAD_W), lambda h, p, s: (qb(p, s), h)),
                  pl.BlockSpec((tq, 2 * HEAD_W), lambda h, p, s: (kb(p, s), h)),
                  pl.BlockSpec((tq, HEAD_W), lambda h, p, s: (kb(p, s), 0))],
        out_specs=[pl.BlockSpec((tq, HEAD_W), lambda h, p, s: (qb(p, s), h)),
                   pl.BlockSpec((None, tq, 1), lambda h, p, s: (h, qb(p, s), 0))],
        out_shape=[jax.ShapeDtypeStruct((t, N_HEADS * HEAD_W), F32),
                   jax.ShapeDtypeStruct((N_HEADS, t, 1), F32)],
        scratch_shapes=[pltpu.VMEM((tq, 1), F32), pltpu.VMEM((tq, 1), F32), pltpu.VMEM((tq, HEAD_W), F32)],
        compiler_params=_params(("parallel", "parallel", "arbitrary")),
    )(proj, qr, kv, kr)


def _flash_dq(proj, qr, kv, kr, do, o, lse):
    t = proj.shape[0]
    tq, nq, n_pairs, n_steps = _attn_tiles(t)

    def body(qn_ref, qr_ref, kv_ref, kr_ref, do_ref, o_ref, lse_ref, dqn_ref, dqr_ref, dq_s, dsum_s):
        qi, ki = _tri_q(pl.program_id(1), pl.program_id(2), nq)

        @pl.when(ki == 0)
        def _():
            dq_s[...] = jnp.zeros_like(dq_s)
            dsum_s[...] = jnp.sum(do_ref[...] * o_ref[...], axis=1, keepdims=True)

        def step(diag):
            q, k = _qk(qn_ref, qr_ref, kv_ref, kr_ref)
            for r0, c in _row_chunks(tq):
                rows = slice(r0, r0 + c)
                n = r0 + c if diag else tq
                p = jnp.exp(_scores(q[rows], k[:n], r0, diag) - lse_ref[rows])
                dp = _dot_nt(do_ref[rows], kv_ref[:n, HEAD_W:])
                ds = p * (dp - dsum_s[rows]) * SOFTMAX_SCALE
                dq_s[rows] += _dot(ds, k[:n])

        @pl.when(ki < qi)
        def _():
            step(False)

        @pl.when(ki == qi)
        def _():
            step(True)
            dqn_ref[...] = dq_s[:, :HEAD_W].astype(dqn_ref.dtype)
            dqr_ref[...] = dq_s[:, HEAD_W:]

    qb = lambda p, s: _tri_q(p, s, nq)[0]
    kb = lambda p, s: _tri_q(p, s, nq)[1]
    qspec = pl.BlockSpec((tq, HEAD_W), lambda h, p, s: (qb(p, s), h))
    return pl.pallas_call(
        body, name="mla_dq", grid=(N_HEADS, n_pairs, n_steps),
        in_specs=[qspec, qspec,
                  pl.BlockSpec((tq, 2 * HEAD_W), lambda h, p, s: (kb(p, s), h)),
                  pl.BlockSpec((tq, HEAD_W), lambda h, p, s: (kb(p, s), 0)),
                  qspec, qspec,
                  pl.BlockSpec((None, tq, 1), lambda h, p, s: (h, qb(p, s), 0))],
        out_specs=[qspec, qspec],
        out_shape=[jax.ShapeDtypeStruct((t, N_HEADS * HEAD_W), BF16),
                   jax.ShapeDtypeStruct((t, N_HEADS * HEAD_W), F32)],
        scratch_shapes=[pltpu.VMEM((tq, 2 * HEAD_W), F32), pltpu.VMEM((tq, 1), F32)],
        compiler_params=_params(("parallel", "parallel", "arbitrary")),
    )(proj, qr, kv, kr, do, o, lse)


def _flash_dkv(proj, qr, kv, kr, do, o, lse):
    t = proj.shape[0]
    tq, nq, n_pairs, n_steps = _attn_tiles(t)

    def body(qn_ref, qr_ref, kv_ref, kr_ref, do_ref, o_ref, lse_ref, dkv_ref, dkr_ref, dk_s, dv_s):
        qi, ki = _tri_k(pl.program_id(1), pl.program_id(2), nq)

        @pl.when(qi == ki)
        def _():
            dk_s[...] = jnp.zeros_like(dk_s)
            dv_s[...] = jnp.zeros_like(dv_s)

        def step(diag):
            q, k = _qk(qn_ref, qr_ref, kv_ref, kr_ref)
            for r0, c in _row_chunks(tq):
                rows = slice(r0, r0 + c)
                n = r0 + c if diag else tq
                p = jnp.exp(_scores(q[rows], k[:n], r0, diag) - lse_ref[rows])
                do = do_ref[rows]
                dsum = jnp.sum(do * o_ref[rows], axis=1, keepdims=True)
                dv_s[:n] += _dot_tn(p, do)
                dp = _dot_nt(do, kv_ref[:n, HEAD_W:])
                ds = p * (dp - dsum) * SOFTMAX_SCALE
                dk_s[:n] += _dot_tn(ds, q[rows])

        @pl.when(qi == ki)
        def _():
            step(True)

        @pl.when(qi > ki)
        def _():
            step(False)

        @pl.when(qi == nq - 1)
        def _():
            dkv_ref[...] = jnp.concatenate([dk_s[:, :HEAD_W], dv_s[...]], axis=1).astype(dkv_ref.dtype)
            dkr_ref[...] = dk_s[:, HEAD_W:]

    qb = lambda p, s: _tri_k(p, s, nq)[0]
    kb = lambda p, s: _tri_k(p, s, nq)[1]
    qspec = pl.BlockSpec((tq, HEAD_W), lambda h, p, s: (qb(p, s), h))
    return pl.pallas_call(
        body, name="mla_dkv", grid=(N_HEADS, n_pairs, n_steps),
        in_specs=[qspec, qspec,
                  pl.BlockSpec((tq, 2 * HEAD_W), lambda h, p, s: (kb(p, s), h)),
                  pl.BlockSpec((tq, HEAD_W), lambda h, p, s: (kb(p, s), 0)),
                  qspec, qspec,
                  pl.BlockSpec((None, tq, 1), lambda h, p, s: (h, qb(p, s), 0))],
        out_specs=[pl.BlockSpec((tq, 2 * HEAD_W), lambda h, p, s: (kb(p, s), h)),
                   pl.BlockSpec((None, tq, HEAD_W), lambda h, p, s: (h, kb(p, s), 0))],
        out_shape=[jax.ShapeDtypeStruct((t, N_HEADS * 2 * HEAD_W), BF16),
                   jax.ShapeDtypeStruct((N_HEADS, t, HEAD_W), F32)],
        scratch_shapes=[pltpu.VMEM((tq, 2 * HEAD_W), F32), pltpu.VMEM((tq, HEAD_W), F32)],
        compiler_params=_params(("parallel", "parallel", "arbitrary")),
    )(proj, qr, kv, kr, do, o, lse)


def _block_diag(x, w):
    return jnp.concatenate(
        [_dot(x[:, b * LANES:(b + 1) * LANES], w[b]) for b in range(LRU_BLOCKS)], axis=1)


def _block_diag_t(d, w):
    return jnp.concatenate(
        [_dot_nt(d[:, b * LANES:(b + 1) * LANES], w[b]) for b in range(LRU_BLOCKS)], axis=1)


def _lru_gates(xc, w_rg, b_rg, w_ig, b_ig, lam):
    r = _sigmoid(_block_diag(xc, w_rg) + b_rg)
    ig = _sigmoid(_block_diag(xc, w_ig) + b_ig)
    sp = jnp.maximum(-lam, 0.0) + jnp.log1p(jnp.exp(-jnp.abs(lam)))
    log_a = -LRU_C * r * sp
    a = jnp.exp(log_a)
    mult = jnp.sqrt(-jnp.tanh(log_a) * (a * a + 1.0))
    return r, ig, sp, a, mult


def _lru_tm(t):
    return min(128, t)


def _halo_spec(tm, width, cb):
    return pl.BlockSpec((SUBLANES, width), lambda i, cb=cb: (jnp.maximum(i * (tm // SUBLANES) - 1, 0), cb))


def _lru_fwd(proj, w_conv, b_conv, w_rg, b_rg, w_ig, b_ig, lam):
    t = proj.shape[0]
    tm = _lru_tm(t)
    w = D_MODEL
    n_scan = int(math.log2(tm))

    def body(lx_ref, halo_ref, ly_ref, wc_ref, bc_ref, wrg_ref, brg_ref, wig_ref, big_ref, lam_ref,
             oli_ref, h_ref, carry_s):
        i = pl.program_id(0)

        @pl.when(i == 0)
        def _():
            carry_s[...] = jnp.zeros_like(carry_s)

        x = lx_ref[...]
        halo = jnp.where(i > 0, halo_ref[...], 0.0)
        xc = _causal_conv(_conv_taps(x, halo, 4), wc_ref[...], bc_ref[...])
        _, ig, _, a, mult = _lru_gates(xc, wrg_ref[...], brg_ref[...], wig_ref[...], big_ref[...], lam_ref[...])
        u = mult * (ig * xc)
        rows = _row_iota(a.shape)
        for s in range(n_scan):
            d = 1 << s
            keep = rows >= d
            a_s = jnp.where(keep, pltpu.roll(a, d, 0), 1.0)
            u_s = jnp.where(keep, pltpu.roll(u, d, 0), 0.0)
            u = a * u_s + u
            a = a * a_s
        h = u + a * carry_s[SUBLANES - 1:SUBLANES, :]
        h_ref[...] = h
        carry_s[...] = h[tm - SUBLANES:, :]
        oli_ref[...] = (h * _gelu(ly_ref[...])).astype(oli_ref.dtype)

    full = lambda arr: pl.BlockSpec(arr.shape, lambda i, nd=arr.ndim: (0,) * nd)
    return pl.pallas_call(
        body, name="lru_fwd", grid=(t // tm,),
        in_specs=[pl.BlockSpec((tm, w), lambda i: (i, SEG_LX)), _halo_spec(tm, w, SEG_LX),
                  pl.BlockSpec((tm, w), lambda i: (i, SEG_LY)),
                  full(w_conv), full(b_conv), full(w_rg), full(b_rg), full(w_ig), full(b_ig), full(lam)],
        out_specs=[pl.BlockSpec((tm, w), lambda i: (i, 0)), pl.BlockSpec((tm, w), lambda i: (i, 0))],
        out_shape=[jax.ShapeDtypeStruct((t, w), BF16), jax.ShapeDtypeStruct((t, w), F32)],
        scratch_shapes=[pltpu.VMEM((SUBLANES, w), F32)],
        compiler_params=_params(("arbitrary",)),
    )(proj, proj, proj, w_conv, b_conv, w_rg, b_rg, w_ig, b_ig, lam)


def _lru_bwd(proj, hl, doli, w_conv, b_conv, w_rg, b_rg, w_ig, b_ig, lam):
    t = proj.shape[0]
    tm = _lru_tm(t)
    nt = t // tm
    w = D_MODEL
    n_scan = int(math.log2(tm))

    def body(lx_ref, lxh_ref, ly_ref, h_ref, hh_ref, doli_ref,
             wc_ref, bc_ref, wrg_ref, brg_ref, wig_ref, big_ref, lam_ref,
             dlx_ref, dly_ref, dwc_ref, dbc_ref, dwrg_ref, dbrg_ref, dwig_ref, dbig_ref, dlam_ref,
             ca_s, cd_s, cx_s):
        i = pl.program_id(0)
        blk = nt - 1 - i

        @pl.when(i == 0)
        def _():
            for r in (ca_s, cd_s, cx_s, dwc_ref, dbc_ref, dwrg_ref, dbrg_ref, dwig_ref, dbig_ref, dlam_ref):
                r[...] = jnp.zeros_like(r)

        x = lx_ref[...]
        halo = jnp.where(blk > 0, lxh_ref[...], 0.0)
        taps = _conv_taps(x, halo, 4)
        wc = wc_ref[...]
        xc = _causal_conv(taps, wc, bc_ref[...])
        w_rg, w_ig, lam_v = wrg_ref[...], wig_ref[...], lam_ref[...]
        r, ig, sp, a, mult = _lru_gates(xc, w_rg, brg_ref[...], w_ig, big_ref[...], lam_v)
        h = h_ref[...]
        h_prev = _shift_down(h, 1, jnp.where(blk > 0, hh_ref[...], 0.0))
        gl, dgl = _gelu_and_grad(ly_ref[...])
        doli = doli_ref[...]
        dly_ref[...] = (doli * h * dgl).astype(dly_ref.dtype)
        acc_b = doli * gl
        acc_a = _shift_up(a, 1, ca_s[...])
        rows = _row_iota(a.shape)
        for s in range(n_scan):
            d = 1 << s
            keep = rows < tm - d
            a_s = jnp.where(keep, pltpu.roll(acc_a, tm - d, 0), 1.0)
            b_s = jnp.where(keep, pltpu.roll(acc_b, tm - d, 0), 0.0)
            acc_b = acc_b + acc_a * b_s
            acc_a = acc_a * a_s
        dht = acc_b + acc_a * cd_s[0:1, :]
        da = dht * h_prev
        dmult = dht * (ig * xc)
        di = dht * mult * xc
        dxc = dht * mult * ig
        dlog_a = da * a - dmult * (a * a) / mult
        dr = dlog_a * (-LRU_C * sp)
        dlam_ref[...] += jnp.sum(dlog_a * r, axis=0, keepdims=True) * (LRU_C * _sigmoid(-lam_v))
        dpr = dr * r * (1.0 - r)
        dpi = di * ig * (1.0 - ig)
        dxc = dxc + _block_diag_t(dpr, w_rg) + _block_diag_t(dpi, w_ig)
        dbrg_ref[...] += jnp.sum(dpr, axis=0, keepdims=True)
        dbig_ref[...] += jnp.sum(dpi, axis=0, keepdims=True)
        for b in range(LRU_BLOCKS):
            sl = slice(b * LANES, (b + 1) * LANES)
            dwrg_ref[b] += _dot_tn(xc[:, sl], dpr[:, sl])
            dwig_ref[b] += _dot_tn(xc[:, sl], dpi[:, sl])
        dbc_ref[...] += jnp.sum(dxc, axis=0, keepdims=True)
        dwc_ref[...] += jnp.concatenate(
            [jnp.sum(dxc * taps[3 - k], axis=0, keepdims=True) for k in range(4)], axis=0)
        cx = cx_s[...]
        dlx = wc[3:4] * dxc
        for s in range(1, 4):
            dlx = dlx + wc[3 - s:4 - s] * _shift_up(dxc, s, cx)
        dlx_ref[...] = dlx.astype(dlx_ref.dtype)
        ca_s[...] = a[:SUBLANES, :]
        cd_s[...] = dht[:SUBLANES, :]
        cx_s[...] = dxc[:SUBLANES, :]

    full = lambda arr: pl.BlockSpec(arr.shape, lambda i, nd=arr.ndim: (0,) * nd)
    rev = lambda cb: pl.BlockSpec((tm, w), lambda i, cb=cb: (nt - 1 - i, cb))
    halo = lambda cb: pl.BlockSpec(
        (SUBLANES, w), lambda i, cb=cb: (jnp.maximum((nt - 1 - i) * (tm // SUBLANES) - 1, 0), cb))
    acc = lambda s: pl.BlockSpec(s, lambda i, nd=len(s): (0,) * nd)
    acc_shapes = [(4, w), (1, w), (LRU_BLOCKS, LANES, LANES), (1, w), (LRU_BLOCKS, LANES, LANES), (1, w), (1, w)]
    return pl.pallas_call(
        body, name="lru_bwd", grid=(nt,),
        in_specs=[rev(SEG_LX), halo(SEG_LX), rev(SEG_LY), rev(0), halo(0), rev(0),
                  full(w_conv), full(b_conv), full(w_rg), full(b_rg), full(w_ig), full(b_ig), full(lam)],
        out_specs=[rev(0), rev(0)] + [acc(s) for s in acc_shapes],
        out_shape=[jax.ShapeDtypeStruct((t, w), BF16), jax.ShapeDtypeStruct((t, w), BF16)]
        + [jax.ShapeDtypeStruct(s, F32) for s in acc_shapes],
        scratch_shapes=[pltpu.VMEM((SUBLANES, w), F32)] * 3,
        compiler_params=_params(("arbitrary",)),
    )(proj, proj, proj, hl, hl, doli, w_conv, b_conv, w_rg, b_rg, w_ig, b_ig, lam)


FFN_TC = 512
FFN_NJ = D_FF // FFN_TC


def _ffn_tm(t):
    return min(512, t)


def _ffn_fwd(up_pre, w_fconv, b_fconv):
    t = up_pre.shape[0]
    tm = _ffn_tm(t)

    def body(g_ref, gh_ref, v_ref, vh_ref, wg_ref, wv_ref, bg_ref, bv_ref, act_ref):
        first = pl.program_id(0) > 0
        gate = _causal_conv(_conv_taps(g_ref[...], jnp.where(first, gh_ref[...], 0.0), 3), wg_ref[...], bg_ref[...])
        val = _causal_conv(_conv_taps(v_ref[...], jnp.where(first, vh_ref[...], 0.0), 3), wv_ref[...], bv_ref[...])
        act_ref[...] = (_gelu(gate) * val).astype(act_ref.dtype)

    blk = lambda off: pl.BlockSpec((tm, FFN_TC), lambda i, j: (i, j + off))
    halo = lambda off: pl.BlockSpec(
        (SUBLANES, FFN_TC), lambda i, j: (jnp.maximum(i * (tm // SUBLANES) - 1, 0), j + off))
    wsp = lambda rows, off: pl.BlockSpec((rows, FFN_TC), lambda i, j: (0, j + off))
    return pl.pallas_call(
        body, name="ffn_conv_fwd", grid=(t // tm, FFN_NJ),
        in_specs=[blk(0), halo(0), blk(FFN_NJ), halo(FFN_NJ), wsp(3, 0), wsp(3, FFN_NJ), wsp(1, 0), wsp(1, FFN_NJ)],
        out_specs=pl.BlockSpec((tm, FFN_TC), lambda i, j: (i, j)),
        out_shape=jax.ShapeDtypeStruct((t, D_FF), BF16),
        compiler_params=_params(("parallel", "parallel")),
    )(up_pre, up_pre, up_pre, up_pre, w_fconv, w_fconv, b_fconv, b_fconv)


def _ffn_bwd_act(up_pre, dact, w_fconv, b_fconv):
    t = up_pre.shape[0]
    tm = _ffn_tm(t)

    def body(g_ref, gh_ref, v_ref, vh_ref, da_ref, wg_ref, wv_ref, bg_ref, bv_ref, dup_ref, dw_ref, db_ref):
        jj, i = pl.program_id(0), pl.program_id(1)

        @pl.when(i == 0)
        def _():
            dw_ref[...] = jnp.zeros_like(dw_ref)
            db_ref[...] = jnp.zeros_like(db_ref)

        g_taps = _conv_taps(g_ref[...], jnp.where(i > 0, gh_ref[...], 0.0), 3)
        v_taps = _conv_taps(v_ref[...], jnp.where(i > 0, vh_ref[...], 0.0), 3)
        gate = _causal_conv(g_taps, wg_ref[...], bg_ref[...])
        val = _causal_conv(v_taps, wv_ref[...], bv_ref[...])
        gl, dgl = _gelu_and_grad(gate)
        dact = da_ref[...]
        is_gate = jj < FFN_NJ
        d = jnp.where(is_gate, dact * val * dgl, dact * gl)
        dup_ref[...] = d
        db_ref[...] += jnp.sum(d, axis=0, keepdims=True)
        dw_ref[...] += jnp.concatenate(
            [jnp.sum(d * jnp.where(is_gate, g_taps[2 - k], v_taps[2 - k]), axis=0, keepdims=True)
             for k in range(3)], axis=0)

    jm = lambda jj: jj % FFN_NJ
    blk = lambda off: pl.BlockSpec((tm, FFN_TC), lambda jj, i: (i, jm(jj) + off))
    halo = lambda off: pl.BlockSpec(
        (SUBLANES, FFN_TC), lambda jj, i: (jnp.maximum(i * (tm // SUBLANES) - 1, 0), jm(jj) + off))
    wsp = lambda rows, off: pl.BlockSpec((rows, FFN_TC), lambda jj, i: (0, jm(jj) + off))
    return pl.pallas_call(
        body, name="ffn_bwd_act", grid=(2 * FFN_NJ, t // tm),
        in_specs=[blk(0), halo(0), blk(FFN_NJ), halo(FFN_NJ), blk(0),
                  wsp(3, 0), wsp(3, FFN_NJ), wsp(1, 0), wsp(1, FFN_NJ)],
        out_specs=[pl.BlockSpec((tm, FFN_TC), lambda jj, i: (i, jj)),
                   pl.BlockSpec((3, FFN_TC), lambda jj, i: (0, jj)),
                   pl.BlockSpec((1, FFN_TC), lambda jj, i: (0, jj))],
        out_shape=[jax.ShapeDtypeStruct((t, 2 * D_FF), F32), jax.ShapeDtypeStruct((3, 2 * D_FF), F32),
                   jax.ShapeDtypeStruct((1, 2 * D_FF), F32)],
        compiler_params=_params(("parallel", "arbitrary")),
    )(up_pre, up_pre, up_pre, up_pre, dact, w_fconv, w_fconv, b_fconv, b_fconv)


def _ffn_bwd_conv(dup, w_fconv):
    t = dup.shape[0]
    tm = _ffn_tm(t)
    nt = t // tm

    def body(d_ref, dh_ref, w_ref, out_ref):
        d = d_ref[...]
        halo = jnp.where(pl.program_id(0) < nt - 1, dh_ref[...], 0.0)
        wv = w_ref[...]
        out = wv[2:3] * d + wv[1:2] * _shift_up(d, 1, halo) + wv[0:1] * _shift_up(d, 2, halo)
        out_ref[...] = out.astype(out_ref.dtype)

    return pl.pallas_call(
        body, name="ffn_bwd_conv", grid=(nt, 2 * FFN_NJ),
        in_specs=[pl.BlockSpec((tm, FFN_TC), lambda i, j: (i, j)),
                  pl.BlockSpec((SUBLANES, FFN_TC),
                               lambda i, j: (jnp.minimum((i + 1) * (tm // SUBLANES), t // SUBLANES - 1), j)),
                  pl.BlockSpec((3, FFN_TC), lambda i, j: (0, j))],
        out_specs=pl.BlockSpec((tm, FFN_TC), lambda i, j: (i, j)),
        out_shape=jax.ShapeDtypeStruct((t, 2 * D_FF), BF16),
        compiler_params=_params(("parallel", "parallel")),
    )(dup, dup, w_fconv)


def _xattn_probs(cq, ck, h):
    sl = slice(h * LANES, (h + 1) * LANES)
    s = _dot_nt(cq[:, sl], ck[:, sl]) * X_SCALE
    e = jnp.exp(s - jnp.max(s, axis=1, keepdims=True))
    return e / jnp.sum(e, axis=1, keepdims=True), sl


def _xattn_fwd_fn(cq, ck, cv):
    outs = []
    for h in range(X_HEADS):
        p, sl = _xattn_probs(cq, ck, h)
        outs.append(_dot(p, cv[:, sl]))
    return (jnp.concatenate(outs, axis=1),)


def _xattn_bwd_fn(cq, dco, ck, cv):
    dcq, dck, dcv = [], [], []
    for h in range(X_HEADS):
        p, sl = _xattn_probs(cq, ck, h)
        dcv.append(_dot_tn(p, dco[:, sl]))
        dp = _dot_nt(dco[:, sl], cv[:, sl])
        ds = p * (dp - jnp.sum(p * dp, axis=1, keepdims=True)) * X_SCALE
        dcq.append(_dot(ds, ck[:, sl]))
        dck.append(_dot_tn(ds, cq[:, sl]))
    return jnp.concatenate(dcq, axis=1), jnp.concatenate(dck, axis=1), jnp.concatenate(dcv, axis=1)


def _perm_w_in(w):
    q = w[:, :3072].reshape(D_MODEL, N_HEADS, 192)
    qn = q[:, :, :128].reshape(D_MODEL, 2048)
    qr = jnp.pad(q[:, :, 128:], ((0, 0), (0, 0), (0, 64))).reshape(D_MODEL, 2048)
    ckv = w[:, 3072:3584]
    kr = jnp.pad(w[:, 3584:3648], ((0, 0), (0, 64)))
    rest = w[:, 3648:]
    pad = jnp.zeros((D_MODEL, PROJ_W - 12928), w.dtype)
    return jnp.concatenate([qn, qr, rest, ckv, kr, pad], axis=1)


def _unperm_dw_in(dw):
    qn = dw[:, :2048].reshape(D_MODEL, N_HEADS, 128)
    qr = dw[:, 2048:4096].reshape(D_MODEL, N_HEADS, 128)[:, :, :64]
    q = jnp.concatenate([qn, qr], axis=2).reshape(D_MODEL, 3072)
    return jnp.concatenate([q, dw[:, 12288:12800], dw[:, 12800:12864], dw[:, 4096:12288]], axis=1)


def _local_step(x, mem, positions, tgt, wb, ws):
    t = x.shape[0]
    n_mem = mem.shape[0]
    tm = min(256, t)
    row = lambda v: v.reshape(1, -1)
    g_pre_mix, g_post_mix, g_ckv = row(ws["g_pre_mix"]), row(ws["g_post_mix"]), row(ws["g_ckv"])
    g_pre_x, g_post_x, g_mem = row(ws["g_pre_x"]), row(ws["g_post_x"]), row(ws["g_mem"])
    g_pre_ffn, g_post_ffn = row(ws["g_pre_ffn"]), row(ws["g_post_ffn"])
    b_conv, lam, b_fconv = row(ws["b_conv_lru"]), row(ws["lru_lambda"]), row(ws["b_fconv"])
    b_rg, b_ig = row(ws["b_rg"]), row(ws["b_ig"])
    w_rg, w_ig = ws["w_rg"].astype(BF16), ws["w_ig"].astype(BF16)
    w_in = _perm_w_in(wb["w_in"])
    w_conv = wb["w_conv_lru"].astype(F32)
    w_fconv = wb["w_fconv"].astype(F32)
    W = D_MODEL

    def dw(name, a, b, owners=1, tn=1024):
        g = _mm(name, a, b, "tn", out_dtype=BF16, owners=owners, tn=tn)
        return g.reshape(8, -1, g.shape[-1])

    cos_t, s1_t, s2_t = _rope_tables(positions.reshape(t, 1), tm)
    (h1,) = _rows("rms_pre_mix", lambda x, g: (_rms(x, g),), t, tm, [(x, W, 0)], [g_pre_mix], [(W, BF16)])
    proj = _mm("proj", h1, w_in, "nn")

    def rope_ckv_fn(qr, ckv, kr, c, s1, s2, g):
        return _rope_heads(qr, c, s1, s2), _rope_group(kr, c, s1, s2), _rms(ckv, g)

    qr, kr, ckvn = _rows("rope_ckv", rope_ckv_fn, t, tm,
                         [(proj, W, SEG_QR), (proj, 512, CKV_BLOCK), (proj, LANES, KR_BLOCK),
                          (cos_t, LANES, 0), (s1_t, LANES, 0), (s2_t, LANES, 0)], [g_ckv],
                         [(W, BF16), (LANES, BF16), (512, BF16)])
    kv = _mm("kv_up", ckvn, wb["w_ukv"], "nn", out_dtype=BF16)
    o, lse = _flash_fwd(proj, qr, kv, kr)
    o_mla = _mm("o_mla", o, wb["w_o_mla"], "nn")
    oli, hl = _lru_fwd(proj, w_conv, b_conv, w_rg, b_rg, w_ig, b_ig, lam)
    o_lru = _mm("o_lru", oli, wb["w_o_lru"], "nn")
    (merged,) = _rows("merge", lambda gm, gl, a, b: (_sigmoid(gm) * a + _sigmoid(gl) * b,), t, tm,
                      [(proj, W, SEG_GM), (proj, W, SEG_GL), (o_mla, W, 0), (o_lru, W, 0)], [], [(W, BF16)])
    z1 = _mm("w_out", merged, wb["w_out"], "nn")

    def post_fn(x, z, g_post, g_pre):
        x1 = x + _rms(z, g_post)
        return x1, _rms(x1, g_pre)

    x1, h2 = _rows("post_mix", post_fn, t, tm, [(x, W, 0), (z1, W, 0)], [g_post_mix, g_pre_x],
                   [(W, F32), (W, BF16)])
    cq = _mm("cq", h2, wb["w_cq"], "nn", out_dtype=BF16)
    (mn,) = _rows("rms_mem", lambda m, g: (_rms(m, g),), n_mem, n_mem, [(mem, W, 0)], [g_mem], [(W, BF16)])
    ck = _mm("ck", mn, wb["w_ck"], "nn", out_dtype=BF16)
    cv = _mm("cv", mn, wb["w_cv"], "nn", out_dtype=BF16)
    (co,) = _rows("xattn_fwd", _xattn_fwd_fn, t, tm, [(cq, 512, 0)], [ck, cv], [(512, BF16)])
    z2 = _mm("w_co", co, wb["w_co"], "nn")
    x2, h3 = _rows("post_x", post_fn, t, tm, [(x1, W, 0), (z2, W, 0)], [g_post_x, g_pre_ffn],
                   [(W, F32), (W, BF16)])
    up_pre = _mm("w_up", h3, wb["w_up"], "nn")
    act = _ffn_fwd(up_pre, w_fconv, b_fconv)
    z3 = _mm("w_down", act, wb["w_down"], "nn")

    def loss_fn(x2, z3, tgt, g):
        err = x2 + _rms(z3, g) - tgt
        dy = err * (1.0 / W)
        dz, dg = _rms_bwd(z3, g, dy)
        part = 0.5 * jnp.sum(err * err) * (1.0 / W)
        return dy, dz, jnp.zeros((SUBLANES, LANES), F32) + part, dg

    dy, dz3, loss_acc, dg_post_ffn = _rows(
        "loss", loss_fn, t, tm, [(x2, W, 0), (z3, W, 0), (tgt, W, 0)], [g_post_ffn],
        [(W, F32), (W, BF16)], [(SUBLANES, LANES), (1, W)])
    grads = {"g_post_ffn": dg_post_ffn}
    dact = _mm("d_act", dz3, wb["w_down"], "nt")
    grads["w_down"] = dw("dw_down", act, dz3)
    dup, grads["w_fconv"], grads["b_fconv"] = _ffn_bwd_act(up_pre, dact, w_fconv, b_fconv)
    dup_pre = _ffn_bwd_conv(dup, w_fconv)
    dh3 = _mm("d_h3", dup_pre, wb["w_up"], "nt")
    grads["w_up"] = dw("dw_up", h3, dup_pre, owners=4, tn=1408)

    def res_bwd_fn(dres, xa, dh, z, g_pre, g_post):
        dxa, dg_pre = _rms_bwd(xa, g_pre, dh)
        dxa = dres + dxa
        dz, dg_post = _rms_bwd(z, g_post, dxa)
        return dxa, dz, dg_pre, dg_post

    dx2, dz2, grads["g_pre_ffn"], grads["g_post_x"] = _rows(
        "bwd_post_x", res_bwd_fn, t, tm, [(dy, W, 0), (x2, W, 0), (dh3, W, 0), (z2, W, 0)],
        [g_pre_ffn, g_post_x], [(W, F32), (W, BF16)], [(1, W), (1, W)])
    dco = _mm("d_co", dz2, wb["w_co"], "nt")
    grads["w_co"] = dw("dw_co", co, dz2, owners=4)
    dcq, dck, dcv = _rows("xattn_bwd", _xattn_bwd_fn, t, tm, [(cq, 512, 0), (dco, 512, 0)], [ck, cv],
                          [(512, BF16)], [(n_mem, 512), (n_mem, 512)])
    dh2 = _mm("d_h2", dcq, wb["w_cq"], "nt")
    grads["w_cq"] = dw("dw_cq", h2, dcq)
    grads["w_ck"] = dw("dw_ck", mn, dck)
    grads["w_cv"] = dw("dw_cv", mn, dcv)
    dmn_k = _mm("d_mn_k", dck, wb["w_ck"], "nt")
    dmn_v = _mm("d_mn_v", dcv, wb["w_cv"], "nt")
    (grads["g_mem"],) = _rows("dg_mem", lambda m, a, b: (jnp.sum((a + b) * m * _rsq(m), axis=0, keepdims=True),),
                              n_mem, n_mem, [(mem, W, 0), (dmn_k, W, 0), (dmn_v, W, 0)], [], [], [(1, W)])
    dx1, dz1, grads["g_pre_x"], grads["g_post_mix"] = _rows(
        "bwd_post_mix", res_bwd_fn, t, tm, [(dx2, W, 0), (x1, W, 0), (dh2, W, 0), (z1, W, 0)],
        [g_pre_x, g_post_mix], [(W, F32), (W, BF16)], [(1, W), (1, W)])
    dmerged = _mm("d_merged", dz1, wb["w_out"], "nt")
    grads["w_out"] = dw("dw_out", merged, dz1)

    def merge_bwd_fn(dm, gm, gl, a, b):
        sm, sl = _sigmoid(gm), _sigmoid(gl)
        return dm * sm, dm * sl, dm * a * sm * (1.0 - sm), dm * b * sl * (1.0 - sl)

    do_mla, do_lru, dgm, dgl = _rows(
        "merge_bwd", merge_bwd_fn, t, tm,
        [(dmerged, W, 0), (proj, W, SEG_GM), (proj, W, SEG_GL), (o_mla, W, 0), (o_lru, W, 0)], [],
        [(W, BF16)] * 4)
    do = _mm("d_o", do_mla, wb["w_o_mla"], "nt")
    grads["w_o_mla"] = dw("dw_o_mla", o, do_mla)
    dqn, dqr_pre = _flash_dq(proj, qr, kv, kr, do, o, lse)
    dkv, dkr_h = _flash_dkv(proj, qr, kv, kr, do, o, lse)
    dckvn = _mm("d_ckvn", dkv, wb["w_ukv"], "nt")
    grads["w_ukv"] = dw("dw_ukv", ckvn, dkv, owners=4)

    def rope_bwd_fn(dqr, dckvn, ckv, c, s1, s2, dkr_h, g):
        dkr = dkr_h[0]
        for h in range(1, N_HEADS):
            dkr = dkr + dkr_h[h]
        dckv, dg = _rms_bwd(ckv, g, dckvn)
        return _rope_heads(dqr, c, -s1, -s2), _rope_group(dkr, c, -s1, -s2), dckv, dg

    dqr, dkr, dckv, grads["g_ckv"] = _rope_bwd(rope_bwd_fn, t, tm, dqr_pre, dckvn, proj, cos_t, s1_t, s2_t, dkr_h, g_ckv)
    doli = _mm("d_oli", do_lru, wb["w_o_lru"], "nt")
    grads["w_o_lru"] = dw("dw_o_lru", oli, do_lru)
    (dlx, dly, grads["w_conv_lru"], grads["b_conv_lru"], grads["w_rg"], grads["b_rg"], grads["w_ig"],
     grads["b_ig"], grads["lru_lambda"]) = _lru_bwd(proj, hl, doli, w_conv, b_conv, w_rg, b_rg, w_ig, b_ig, lam)
    dproj = jnp.concatenate([dqn, dqr, dlx, dly, dgm, dgl, dckv, dkr,
                             jnp.zeros((t, PROJ_W - 12928), BF16)], axis=1)
    dh1 = _mm("d_h1", dproj, w_in, "nt")
    dw_in = _unperm_dw_in(_mm("dw_in", h1, dproj, "tn", out_dtype=BF16))
    grads["w_in"] = dw_in.reshape(W, 4, -1).transpose(1, 0, 2).reshape(8, W // 2, -1)

    def in_bwd_fn(dres, x, dh, g):
        dx, dg = _rms_bwd(x, g, dh)
        return dres + dx, dg

    grad_x, grads["g_pre_mix"] = _rows("bwd_pre_mix", in_bwd_fn, t, tm, [(dx1, W, 0), (x, W, 0), (dh1, W, 0)],
                                       [g_pre_mix], [(W, F32)], [(1, W)])
    return loss_acc[0, 0], grad_x, grads


def _rope_bwd(fn, t, tm, dqr_pre, dckvn, proj, cos_t, s1_t, s2_t, dkr_h, g_ckv):
    W = D_MODEL

    def body(dqr_ref, dck_ref, ckv_ref, c_ref, s1_ref, s2_ref, dkrh_ref, g_ref, o1, o2, o3, o4):
        r1, r2, r3, dg = fn(dqr_ref[...], dck_ref[...], ckv_ref[...], c_ref[...], s1_ref[...], s2_ref[...],
                            dkrh_ref[...], g_ref[...])
        o1[...] = r1.astype(o1.dtype)
        o2[...] = r2.astype(o2.dtype)
        o3[...] = r3.astype(o3.dtype)

        @pl.when(pl.program_id(0) == 0)
        def _():
            o4[...] = jnp.zeros_like(o4)

        o4[...] += dg

    rb = lambda w, cb=0: pl.BlockSpec((tm, w), lambda i, cb=cb: (i, cb))
    return pl.pallas_call(
        body, name="rope_bwd", grid=(t // tm,),
        in_specs=[rb(W), rb(512), rb(512, CKV_BLOCK), rb(LANES), rb(LANES), rb(LANES),
                  pl.BlockSpec((N_HEADS, tm, LANES), lambda i: (0, i, 0)),
                  pl.BlockSpec((1, 512), lambda i: (0, 0))],
        out_specs=[rb(W), rb(LANES), rb(512), pl.BlockSpec((1, 512), lambda i: (0, 0))],
        out_shape=[jax.ShapeDtypeStruct((t, W), BF16), jax.ShapeDtypeStruct((t, LANES), BF16),
                   jax.ShapeDtypeStruct((t, 512), BF16), jax.ShapeDtypeStruct((1, 512), F32)],
        compiler_params=_params(("arbitrary",)),
    )(dqr_pre, dckvn, proj, cos_t, s1_t, s2_t, dkr_h, g_ckv)


HBM = pl.BlockSpec(memory_space=pl.ANY)


def _me():
    return lax.axis_index("x"), lax.axis_index("y"), lax.axis_index("c")


def _gather_chips(arrs):
    n = len(arrs)

    def body(*refs):
        srcs, outs = refs[:n], refs[n:2 * n]
        send_sems, recv_sems, local_sems = refs[2 * n:]
        x, y, c = _me()
        chips = [(1 - x, y), (x, 1 - y), (1 - x, 1 - y)]

        def copy(i, k, slot, to):
            return pltpu.make_async_remote_copy(
                src_ref=srcs[i], dst_ref=outs[i].at[slot], send_sem=send_sems.at[i, k], recv_sem=recv_sems.at[i, k],
                device_id=to, device_id_type=MESH)

        local = [pltpu.make_async_copy(srcs[i], outs[i].at[2 * x + y], local_sems.at[i]) for i in range(n)]
        sends = [copy(i, k, 2 * x + y, (px, py, c)) for i in range(n) for k, (px, py) in enumerate(chips)]
        for cp in local + sends:
            cp.start()
        for i in range(n):
            for k, (px, py) in enumerate(chips):
                copy(i, k, 2 * px + py, (px, py, c)).wait_recv()
        for cp in sends:
            cp.wait_send()
        for cp in local:
            cp.wait()

    return pl.pallas_call(
        body, name="gather_weights", in_specs=[HBM] * n, out_specs=[HBM] * n,
        out_shape=[jax.ShapeDtypeStruct((4,) + a.shape, a.dtype) for a in arrs],
        scratch_shapes=[pltpu.SemaphoreType.DMA((n, 3)), pltpu.SemaphoreType.DMA((n, 3)), pltpu.SemaphoreType.DMA((n,))],
    )(*arrs)


def _exchange_all(name, arrs, scatter):
    n = len(arrs)

    def body(*refs):
        srcs, outs = refs[:n], refs[n:2 * n]
        send_sems, recv_sems, local_sems = refs[2 * n:]
        x, y, c = _me()
        me = 4 * x + 2 * y + c

        def copy(i, q):
            return pltpu.make_async_remote_copy(
                src_ref=srcs[i].at[q] if scatter else srcs[i], dst_ref=outs[i].at[me],
                send_sem=send_sems.at[i, q], recv_sem=recv_sems.at[i, me],
                device_id=(q // 4, (q // 2) % 2, q % 2), device_id_type=MESH)

        def arrival(i, s):
            return pltpu.make_async_remote_copy(
                src_ref=srcs[i].at[s] if scatter else srcs[i], dst_ref=outs[i].at[s],
                send_sem=send_sems.at[i, s], recv_sem=recv_sems.at[i, s],
                device_id=(s // 4, (s // 2) % 2, s % 2), device_id_type=MESH)

        local = [pltpu.make_async_copy(srcs[i].at[me] if scatter else srcs[i], outs[i].at[me], local_sems.at[i])
                 for i in range(n)]
        for cp in local:
            cp.start()
        for q in range(8):
            @pl.when(me != q)
            def _(q=q):
                for i in range(n):
                    copy(i, q).start()
        for s in range(8):
            @pl.when(me != s)
            def _(s=s):
                for i in range(n):
                    arrival(i, s).wait_recv()
        for q in range(8):
            @pl.when(me != q)
            def _(q=q):
                for i in range(n):
                    copy(i, q).wait_send()
        for cp in local:
            cp.wait()

    return pl.pallas_call(
        body, name=name, in_specs=[HBM] * n, out_specs=[HBM] * n,
        out_shape=[jax.ShapeDtypeStruct((8,) + (a.shape[1:] if scatter else a.shape), a.dtype) for a in arrs],
        scratch_shapes=[pltpu.SemaphoreType.DMA((n, 8)), pltpu.SemaphoreType.DMA((n, 8)), pltpu.SemaphoreType.DMA((n,))],
    )(*arrs)


D2D_CHUNK_BYTES = 2 * 1024 * 1024


def _gather_cores(arrs):
    n = len(arrs)
    chunks = []
    for i, a in enumerate(arrs):
        r = a.shape[0]
        k = 1
        while a.size * a.dtype.itemsize // k > D2D_CHUNK_BYTES and r % (2 * k) == 0 and (r // (2 * k)) % SUBLANES == 0:
            k *= 2
        chunks += [(i, j * (r // k), r // k) for j in range(k)]
    nc = len(chunks)

    def body(*refs):
        srcs, outs = refs[:n], refs[n:2 * n]
        send_sems, recv_sems, local_sems = refs[2 * n:]
        x, y, c = _me()

        def copy(j, slot):
            i, r0, nr = chunks[j]
            return pltpu.make_async_remote_copy(
                src_ref=srcs[i].at[pl.ds(r0, nr)], dst_ref=outs[i].at[slot, pl.ds(r0, nr)],
                send_sem=send_sems.at[j], recv_sem=recv_sems.at[j],
                device_id=(x, y, 1 - c), device_id_type=MESH)

        local = [pltpu.make_async_copy(srcs[i].at[pl.ds(r0, nr)], outs[i].at[c, pl.ds(r0, nr)], local_sems.at[j])
                 for j, (i, r0, nr) in enumerate(chunks)]
        sends = [copy(j, c) for j in range(nc)]
        for cp in sends:
            cp.start()
        for cp in local:
            cp.start()
        for j in range(nc):
            copy(j, 1 - c).wait_recv()
        for cp in sends:
            cp.wait_send()
        for cp in local:
            cp.wait()

    return pl.pallas_call(
        body, name="gather_cores", in_specs=[HBM] * n, out_specs=[HBM] * n,
        out_shape=[jax.ShapeDtypeStruct((2,) + a.shape, a.dtype) for a in arrs],
        scratch_shapes=[pltpu.SemaphoreType.DMA((nc,))] * 3,
    )(*arrs)


def _row_tile(rows, row_bytes, budget):
    for t in (2048, 1024, 512, 256, 128, 64, 32, 16, 8):
        if rows % t == 0 and t * row_bytes <= budget:
            return t
    return rows


def _sum_slots(name, buf):
    _, r, w = buf.shape
    tr = _row_tile(r, 8 * w * buf.dtype.itemsize, 8 * 1024 * 1024)

    def body(b_ref, o_ref):
        acc = b_ref[0].astype(F32)
        for s in range(1, 8):
            acc = acc + b_ref[s].astype(F32)
        o_ref[...] = acc

    return pl.pallas_call(
        body, name=name, grid=(r // tr,),
        in_specs=[pl.BlockSpec((8, tr, w), lambda i: (0, i, 0))],
        out_specs=pl.BlockSpec((tr, w), lambda i: (i, 0)),
        out_shape=jax.ShapeDtypeStruct((r, w), F32),
        compiler_params=_params(("parallel",)),
    )(buf)


def _adamw(name, w, g, m, v, tr):
    def fn(w, g, m, v):
        m = ADAM_B1 * m + (1.0 - ADAM_B1) * g
        v = ADAM_B2 * v + (1.0 - ADAM_B2) * (g * g)
        m_hat = m / (1.0 - ADAM_B1 ** ADAM_STEP)
        v_hat = v / (1.0 - ADAM_B2 ** ADAM_STEP)
        delta = -ADAM_LR * (m_hat / (jnp.sqrt(v_hat) + ADAM_EPS) + ADAM_WD * w)
        return delta, m, v

    r, c = w.shape
    return _rows(name, fn, r, tr, [(a, c, 0) for a in (w, g, m, v)], [], [(c, F32)] * 3)


def _pack(arrays, n_rows, dtype):
    flat = jnp.concatenate([a.reshape(-1).astype(dtype) for a in arrays])
    return jnp.pad(flat, (0, n_rows * FLAT_W - flat.shape[0])).reshape(n_rows, FLAT_W)


def _unpack(flat, shapes):
    flat = flat.reshape(-1)
    out, off = [], 0
    for s in shapes:
        n = math.prod(s)
        out.append(flat[off:off + n].reshape(s))
        off += n
    return out


def kernel(x, mem, positions, g_pre_mix, g_post_mix, w_in, g_ckv, w_ukv, w_o_mla, w_conv_lru, b_conv_lru, w_rg, b_rg, w_ig, b_ig, lru_lambda, w_o_lru, w_out, g_pre_x, g_post_x, g_mem, w_cq, w_ck, w_cv, w_co, g_pre_ffn, g_post_ffn, w_up, w_fconv, b_fconv, w_down, loss_target, m_g_pre_mix, m_g_post_mix, m_w_in, m_g_ckv, m_w_ukv, m_w_o_mla, m_w_conv_lru, m_b_conv_lru, m_w_rg, m_b_rg, m_w_ig, m_b_ig, m_lru_lambda, m_w_o_lru, m_w_out, m_g_pre_x, m_g_post_x, m_g_mem, m_w_cq, m_w_ck, m_w_cv, m_w_co, m_g_pre_ffn, m_g_post_ffn, m_w_up, m_w_fconv, m_b_fconv, m_w_down, v_g_pre_mix, v_g_post_mix, v_w_in, v_g_ckv, v_w_ukv, v_w_o_mla, v_w_conv_lru, v_b_conv_lru, v_w_rg, v_b_rg, v_w_ig, v_b_ig, v_lru_lambda, v_w_o_lru, v_w_out, v_g_pre_x, v_g_post_x, v_g_mem, v_w_cq, v_w_ck, v_w_cv, v_w_co, v_g_pre_ffn, v_g_post_ffn, v_w_up, v_w_fconv, v_b_fconv, v_w_down):
    given = dict(locals())
    small_names = [n for n, _ in SMALL]
    conv_names = [n for n, _, _ in CONV]

    mine = [given[n][0].astype(BF16) for n, _, _ in BIG] + [given[n][0] for n in conv_names]
    gathered = _gather_chips(mine)
    wb = {}
    for (n, _, ax), g in zip(BIG + CONV, gathered):
        wb[n] = g.reshape(-1, g.shape[-1]) if ax == 0 else jnp.concatenate([g[j] for j in range(4)], axis=1)
    ws = {n: given[n][0] for n in small_names}

    loss_part, grad_x, grads = _local_step(x[0], mem[0], positions[0], loss_target[0], wb, ws)
    loss = lax.psum(loss_part, ("x", "y", "c"))

    landed = _exchange_all("scatter_grads", [grads[n] for n, _, _ in BIG], scatter=True)
    pieces = [_sum_slots("reduce_" + n, buf) for (n, _, _), buf in zip(BIG, landed)]
    g_big = {n: g.reshape(s) for (n, s, _), g in zip(BIG, _gather_cores(pieces))}

    xs_shapes = [s for _, s in SMALL] + [(s[0], 4 * s[1]) for _, s, _ in CONV]
    s_part = _pack([grads[n].reshape(s) for n, s in zip(small_names + conv_names, xs_shapes)], SMALL_XCHG_ROWS, F32)
    (s_all,) = _exchange_all("gather_small", [s_part], scatter=False)
    g_small = dict(zip(small_names + conv_names, _unpack(_sum_slots("reduce_small", s_all), xs_shapes)))
    chip = 2 * lax.axis_index("x") + lax.axis_index("y")
    for n, s, _ in CONV:
        g_small[n] = lax.dynamic_slice_in_dim(g_small[n], chip * s[1], s[1], axis=1)

    out = {n: (g_big[n],) + tuple(_adamw("adamw_" + n, given[n][0], g_big[n], given["m_" + n][0], given["v_" + n][0],
                                         _row_tile(s[0], 4 * s[1], 1536 * 1024))) for n, s, _ in BIG}
    sm_shapes = [s for _, s in SMALL] + [s for _, s, _ in CONV]
    pack_small = lambda pre: _pack([given[pre + n][0] for n in small_names + conv_names], SMALL_ROWS, F32)
    g_flat = _pack([g_small[n] for n in small_names + conv_names], SMALL_ROWS, F32)
    small_out = (g_flat,) + tuple(_adamw("adamw_small", pack_small(""), g_flat, pack_small("m_"), pack_small("v_"), 280))
    for kind in range(4):
        for n, a in zip(small_names + conv_names, _unpack(small_out[kind], sm_shapes)):
            out.setdefault(n, [None] * 4)
            out[n] = list(out[n])
            out[n][kind] = a
    outs = [out[n][kind][None] for kind in range(4) for n in WEIGHTS]
    return (loss, grad_x[None], *outs)
```

```python
import functools
import math

import jax
import jax.numpy as jnp
from jax import lax
from jax.experimental import pallas as pl
from jax.experimental.pallas import tpu as pltpu

F32 = jnp.float32
BF16 = jnp.bfloat16
MESH = pl.DeviceIdType.MESH

EPS = 1e-6
D_MODEL = 2048
N_HEADS = 16
HEAD_W = 128
ROPE_HALF = 32
SOFTMAX_SCALE = 192 ** -0.5
ROPE_THETA = 10000.0
LRU_BLOCKS = 16
LRU_C = 8.0
X_HEADS = 4
X_SCALE = 128 ** -0.5
D_FF = 5632

ADAM_LR = 0.001
ADAM_B1 = 0.9
ADAM_B2 = 0.999
ADAM_EPS = 1e-08
ADAM_WD = 0.01
ADAM_STEP = 10

V7X_VMEM_LIMIT = 56 * 1024 * 1024
LANES = 128
SUBLANES = 8
ATTN_TILE = 1024
ATTN_CHUNK = 256
MM_BLOCK_BYTES = 4 * 1024 * 1024

PROJ_W = 13312
SEG_QN, SEG_QR, SEG_LX, SEG_LY, SEG_GM, SEG_GL = range(6)
CKV_BLOCK = 24
KR_BLOCK = 100

BIG = (
    ("w_in", (2048, 2960), 1), ("w_ukv", (512, 1024), 1), ("w_o_mla", (512, 2048), 0),
    ("w_o_lru", (512, 2048), 0), ("w_out", (512, 2048), 0),
    ("w_cq", (512, 512), 0), ("w_ck", (512, 512), 0), ("w_cv", (512, 512), 0),
    ("w_co", (512, 512), 1), ("w_up", (2048, 2816), 1), ("w_down", (1408, 2048), 0),
)
CONV = (("w_conv_lru", (4, 512), 1), ("w_fconv", (3, 2816), 1))
FIRST_WEIGHTS = ("w_in", "w_ukv")
EARLY_GRADS = ("w_down", "w_up", "w_co", "w_cq", "w_ck", "w_cv", "w_out", "w_o_mla")
SMALL = (
    ("g_pre_mix", (2048,)), ("g_post_mix", (2048,)), ("g_ckv", (512,)), ("b_conv_lru", (2048,)),
    ("w_rg", (16, 128, 128)), ("b_rg", (16, 128)), ("w_ig", (16, 128, 128)), ("b_ig", (16, 128)),
    ("lru_lambda", (2048,)), ("g_pre_x", (2048,)), ("g_post_x", (2048,)), ("g_mem", (2048,)),
    ("g_pre_ffn", (2048,)), ("g_post_ffn", (2048,)), ("b_fconv", (11264,)),
)
WEIGHTS = ("g_pre_mix", "g_post_mix", "w_in", "g_ckv", "w_ukv", "w_o_mla", "w_conv_lru", "b_conv_lru",
           "w_rg", "b_rg", "w_ig", "b_ig", "lru_lambda", "w_o_lru", "w_out", "g_pre_x", "g_post_x", "g_mem",
           "w_cq", "w_ck", "w_cv", "w_co", "g_pre_ffn", "g_post_ffn", "w_up", "w_fconv", "b_fconv", "w_down")
FLAT_W = 1024
SMALL_XCHG_ROWS = 640
SMALL_ROWS = 560


def _params(sem=None):
    return pltpu.CompilerParams(dimension_semantics=sem, vmem_limit_bytes=V7X_VMEM_LIMIT)


def _tile(dim, pref):
    if dim <= pref:
        return dim
    for t in range(pref - pref % LANES, 0, -LANES):
        if dim % t == 0:
            return t
    raise ValueError((dim, pref))


def _mm(name, a, b, mode, out_dtype=F32, tm=1024, tn=1024, tk=None, owners=1):
    if mode == "nn":
        (m, k), (_, n) = a.shape, b.shape
    elif mode == "nt":
        (m, k), (n, _) = a.shape, b.shape
    else:
        (k, m), (_, n) = a.shape, b.shape
    tm, tn = _tile(m, tm), _tile(n // owners, tn)
    if tk is None:
        tk = MM_BLOCK_BYTES // max(tm * a.dtype.itemsize, tn * b.dtype.itemsize)
    tk = _tile(k, tk)
    nk = k // tk
    per_owner = n // owners // tn
    if owners == 1:
        out_spec = pl.BlockSpec((tm, tn), lambda i, j, kk: (i, j))
        out_shape = jax.ShapeDtypeStruct((m, n), out_dtype)
    else:
        out_spec = pl.BlockSpec((None, tm, tn), lambda i, j, kk: (j // per_owner, i, j % per_owner))
        out_shape = jax.ShapeDtypeStruct((owners, m, n // owners), out_dtype)
    if mode == "nn":
        a_spec = pl.BlockSpec((tm, tk), lambda i, j, kk: (i, kk))
        b_spec = pl.BlockSpec((tk, tn), lambda i, j, kk: (kk, j))
        dims = (((1,), (0,)), ((), ()))
    elif mode == "nt":
        a_spec = pl.BlockSpec((tm, tk), lambda i, j, kk: (i, kk))
        b_spec = pl.BlockSpec((tn, tk), lambda i, j, kk: (j, kk))
        dims = (((1,), (1,)), ((), ()))
    else:
        a_spec = pl.BlockSpec((tk, tm), lambda i, j, kk: (kk, i))
        b_spec = pl.BlockSpec((tk, tn), lambda i, j, kk: (kk, j))
        dims = (((0,), (0,)), ((), ()))

    def body(a_ref, b_ref, o_ref, acc_ref):
        kk = pl.program_id(2)

        @pl.when(kk == 0)
        def _():
            acc_ref[...] = jnp.zeros_like(acc_ref)

        acc_ref[...] += lax.dot_general(a_ref[...].astype(BF16), b_ref[...].astype(BF16), dims,
                                        preferred_element_type=F32)

        @pl.when(kk == nk - 1)
        def _():
            o_ref[...] = acc_ref[...].astype(o_ref.dtype)

    return pl.pallas_call(
        body, name=name, grid=(m // tm, n // tn, nk),
        in_specs=[a_spec, b_spec], out_specs=out_spec, out_shape=out_shape,
        scratch_shapes=[pltpu.VMEM((tm, tn), F32)],
        compiler_params=_params(("parallel", "parallel", "arbitrary")),
    )(a, b)


def _rows(name, fn, n_rows, tm, row_ins, full_ins, row_outs, acc_outs=()):
    in_specs, args = [], []
    for arr, width, cb in row_ins:
        in_specs.append(pl.BlockSpec((tm, width), lambda i, cb=cb: (i, cb)))
        args.append(arr)
    for arr in full_ins:
        in_specs.append(pl.BlockSpec(arr.shape, lambda i, nd=arr.ndim: (0,) * nd))
        args.append(arr)
    out_shape = [jax.ShapeDtypeStruct((n_rows, w), dt) for w, dt in row_outs]
    out_specs = [pl.BlockSpec((tm, w), lambda i: (i, 0)) for w, _ in row_outs]
    for s in acc_outs:
        out_shape.append(jax.ShapeDtypeStruct(s, F32))
        out_specs.append(pl.BlockSpec(s, lambda i, nd=len(s): (0,) * nd))
    n_in, n_ro = len(args), len(row_outs)

    def body(*refs):
        res = fn(*[r[...] for r in refs[:n_in]])
        outs = refs[n_in:]
        for r, v in zip(outs[:n_ro], res[:n_ro]):
            r[...] = v.astype(r.dtype)
        first = pl.program_id(0) == 0
        for r, v in zip(outs[n_ro:], res[n_ro:]):
            @pl.when(first)
            def _(r=r):
                r[...] = jnp.zeros_like(r)

            r[...] += v

    return pl.pallas_call(
        body, name=name, grid=(n_rows // tm,), in_specs=in_specs, out_specs=out_specs, out_shape=out_shape,
        compiler_params=_params(("arbitrary",)),
    )(*args)


def _rsq(x):
    return lax.rsqrt(jnp.mean(x * x, axis=-1, keepdims=True) + EPS)


def _rms(x, g):
    return x * _rsq(x) * g


def _rms_bwd(x, g, dy):
    r = _rsq(x)
    xh = x * r
    dxh = dy * g
    dx = r * (dxh - xh * jnp.mean(dxh * xh, axis=-1, keepdims=True))
    return dx, jnp.sum(dy * xh, axis=0, keepdims=True)


def _sigmoid(x):
    return 1.0 / (1.0 + jnp.exp(-x))


_GELU_C = math.sqrt(2.0 / math.pi)


def _gelu(x):
    return 0.5 * x * (1.0 + jnp.tanh(_GELU_C * (x + 0.044715 * x * x * x)))


def _gelu_and_grad(x):
    th = jnp.tanh(_GELU_C * (x + 0.044715 * x * x * x))
    g = 0.5 * x * (1.0 + th)
    dg = 0.5 * (1.0 + th) + 0.5 * x * (1.0 - th * th) * _GELU_C * (1.0 + 3.0 * 0.044715 * x * x)
    return g, dg


def _dot(a, b):
    return jnp.dot(a.astype(BF16), b.astype(BF16), preferred_element_type=F32)


def _dot_nt(a, b):
    return lax.dot_general(a.astype(BF16), b.astype(BF16), (((1,), (1,)), ((), ())), preferred_element_type=F32)


def _dot_tn(a, b):
    return lax.dot_general(a.astype(BF16), b.astype(BF16), (((0,), (0,)), ((), ())), preferred_element_type=F32)


def _row_iota(shape):
    return lax.broadcasted_iota(jnp.int32, shape, 0)


def _shift_down(x, k, halo):
    xs = pltpu.roll(x, k, 0)
    hs = pltpu.roll(halo, k, 0)
    first = jnp.where(_row_iota(hs.shape) < k, hs, xs[:SUBLANES])
    return jnp.concatenate([first, xs[SUBLANES:]], axis=0)


def _shift_up(x, k, halo):
    tm = x.shape[0]
    xs = pltpu.roll(x, tm - k, 0)
    hs = pltpu.roll(halo, SUBLANES - k, 0)
    last = jnp.where(_row_iota(hs.shape) >= SUBLANES - k, hs, xs[tm - SUBLANES:])
    return jnp.concatenate([xs[:tm - SUBLANES], last], axis=0)


def _conv_taps(x, halo, n_taps):
    return [x] + [_shift_down(x, k, halo) for k in range(1, n_taps)]


def _causal_conv(taps, w, b):
    kw = len(taps)
    y = b + w[kw - 1:kw] * taps[0]
    for s in range(1, kw):
        y = y + w[kw - 1 - s:kw - s] * taps[s]
    return y


def _rope_group(g, c, s1, s2):
    return g * c + pltpu.roll(g, LANES - ROPE_HALF, 1) * s1 + pltpu.roll(g, ROPE_HALF, 1) * s2


def _rope_heads(q, c, s1, s2):
    return jnp.concatenate(
        [_rope_group(q[:, h * HEAD_W:(h + 1) * HEAD_W], c, s1, s2) for h in range(N_HEADS)], axis=1)


def _rope_tables(pos_col, tm):
    inv_freq = ROPE_THETA ** (-jnp.arange(0, 2 * ROPE_HALF, 2, dtype=F32) / (2 * ROPE_HALF))
    invf = jnp.concatenate([inv_freq, inv_freq, jnp.zeros((LANES - 2 * ROPE_HALF,), F32)])[None, :]

    def fn(pos, invf):
        ang = pos.astype(F32) * invf
        c, s = jnp.cos(ang), jnp.sin(ang)
        lane = lax.broadcasted_iota(jnp.int32, ang.shape, 1)
        s1 = jnp.where(lane < ROPE_HALF, -s, 0.0)
        s2 = jnp.where((lane >= ROPE_HALF) & (lane < 2 * ROPE_HALF), s, 0.0)
        return c, s1, s2

    n = pos_col.shape[0]
    return _rows("rope_tables", fn, n, tm, [(pos_col, 1, 0)], [invf], [(LANES, F32)] * 3)


def _attn_tiles(t):
    tq = min(ATTN_TILE, t)
    nq = t // tq
    assert nq == 1 or nq % 2 == 0
    return tq, nq, max(nq // 2, 1), (nq + 1 if nq > 1 else 1)


def _row_chunks(tq):
    c = min(ATTN_CHUNK, tq)
    return [(r0, c) for r0 in range(0, tq, c)]


def _tri_q(p, s, nq):
    first = s <= p
    return jnp.where(first, p, nq - 1 - p), jnp.where(first, s, s - p - 1)


def _tri_k(p, s, nq):
    first = s < nq - p
    return jnp.where(first, p + s, s - 1), jnp.where(first, p, nq - 1 - p)


def _qk(qn_ref, qr_ref, kv_ref, kr_ref):
    q = jnp.concatenate([qn_ref[...].astype(BF16), qr_ref[...]], axis=1)
    k = jnp.concatenate([kv_ref[:, :HEAD_W], kr_ref[...]], axis=1)
    return q, k


def _scores(q, k, r0, diag):
    s = _dot_nt(q, k) * SOFTMAX_SCALE
    if diag:
        row = r0 + lax.broadcasted_iota(jnp.int32, s.shape, 0)
        col = lax.broadcasted_iota(jnp.int32, s.shape, 1)
        s = jnp.where(col <= row, s, -1e30)
    return s


def _with_rider(body, n_in, n_out, n_scratch, n, make_ops, grid):
    def wrapped(*refs):
        a, b, c, d = n_in + n, n_in + n + n_out, n_in + 2 * n + n_out, n_in + 2 * n + n_out + n_scratch
        start, finish = make_ops(refs[n_in:a], refs[b:c], *refs[d:])
        ids = [pl.program_id(ax) for ax in range(len(grid))]
        first = functools.reduce(jnp.logical_and, [i == 0 for i in ids])
        last = functools.reduce(jnp.logical_and, [i == g - 1 for i, g in zip(ids, grid)])

        @pl.when(first)
        def _():
            start()

        body(*refs[:n_in], *refs[a:b], *refs[c:d])

        @pl.when(last)
        def _():
            finish()

    return wrapped


def _flash_fwd(proj, qr, kv, kr, ride=()):
    t = proj.shape[0]
    tq, nq, n_pairs, n_steps = _attn_tiles(t)
    n = len(ride)
    grid = (N_HEADS, n_pairs, n_steps)

    def body(qn_ref, qr_ref, kv_ref, kr_ref, o_ref, lse_ref, m_s, l_s, acc_s):
        qi, ki = _tri_q(pl.program_id(1), pl.program_id(2), nq)

        @pl.when(ki == 0)
        def _():
            m_s[...] = jnp.full_like(m_s, -1e30)
            l_s[...] = jnp.zeros_like(l_s)
            acc_s[...] = jnp.zeros_like(acc_s)

        def step(diag):
            q, k = _qk(qn_ref, qr_ref, kv_ref, kr_ref)
            for r0, c in _row_chunks(tq):
                rows = slice(r0, r0 + c)
                n = r0 + c if diag else tq
                s = _scores(q[rows], k[:n], r0, diag)
                m_prev = m_s[rows]
                m_new = jnp.maximum(m_prev, jnp.max(s, axis=1, keepdims=True))
                alpha = jnp.exp(m_prev - m_new)
                p = jnp.exp(s - m_new)
                l_s[rows] = alpha * l_s[rows] + jnp.sum(p, axis=1, keepdims=True)
                acc_s[rows] = alpha * acc_s[rows] + _dot(p, kv_ref[:n, HEAD_W:])
                m_s[rows] = m_new

        @pl.when(ki < qi)
        def _():
            step(False)

        @pl.when(ki == qi)
        def _():
            step(True)
            o_ref[...] = acc_s[...] / l_s[...]
            lse_ref[...] = m_s[...] + jnp.log(l_s[...])

    qb = lambda p, s: _tri_q(p, s, nq)[0]
    kb = lambda p, s: _tri_q(p, s, nq)[1]
    return pl.pallas_call(
        _with_rider(body, 4, 2, 3, n, _gather_chips_ops, grid) if n else body, name="mla_fwd", grid=grid,
        in_specs=[pl.BlockSpec((tq, HEAD_W), lambda h, p, s: (qb(p, s), h)),
                  pl.BlockSpec((tq, HEAD_W), lambda h, p, s: (qb(p, s), h)),
                  pl.BlockSpec((tq, 2 * HEAD_W), lambda h, p, s: (kb(p, s), h)),
                  pl.BlockSpec((tq, HEAD_W), lambda h, p, s: (kb(p, s), 0))] + [HBM] * n,
        out_specs=[pl.BlockSpec((tq, HEAD_W), lambda h, p, s: (qb(p, s), h)),
                   pl.BlockSpec((None, tq, 1), lambda h, p, s: (h, qb(p, s), 0))] + [HBM] * n,
        out_shape=[jax.ShapeDtypeStruct((t, N_HEADS * HEAD_W), F32),
                   jax.ShapeDtypeStruct((N_HEADS, t, 1), F32)]
        + [jax.ShapeDtypeStruct((4,) + a.shape, a.dtype) for a in ride],
        scratch_shapes=[pltpu.VMEM((tq, 1), F32), pltpu.VMEM((tq, 1), F32), pltpu.VMEM((tq, HEAD_W), F32)]
        + (_gather_sems(n) if n else []),
        compiler_params=_params(("arbitrary",) * 3 if n else ("parallel", "parallel", "arbitrary")),
    )(proj, qr, kv, kr, *ride)


def _flash_dq(proj, qr, kv, kr, do, o, lse, ride=()):
    t = proj.shape[0]
    tq, nq, n_pairs, n_steps = _attn_tiles(t)
    n = len(ride)
    grid = (N_HEADS, n_pairs, n_steps)
    scatter_ops = functools.partial(_exchange_ops, scatter=True)

    def body(qn_ref, qr_ref, kv_ref, kr_ref, do_ref, o_ref, lse_ref, dqn_ref, dqr_ref, dq_s, dsum_s):
        qi, ki = _tri_q(pl.program_id(1), pl.program_id(2), nq)

        @pl.when(ki == 0)
        def _():
            dq_s[...] = jnp.zeros_like(dq_s)
            dsum_s[...] = jnp.sum(do_ref[...] * o_ref[...], axis=1, keepdims=True)

        def step(diag):
            q, k = _qk(qn_ref, qr_ref, kv_ref, kr_ref)
            for r0, c in _row_chunks(tq):
                rows = slice(r0, r0 + c)
                n = r0 + c if diag else tq
                p = jnp.exp(_scores(q[rows], k[:n], r0, diag) - lse_ref[rows])
                dp = _dot_nt(do_ref[rows], kv_ref[:n, HEAD_W:])
                ds = p * (dp - dsum_s[rows]) * SOFTMAX_SCALE
                dq_s[rows] += _dot(ds, k[:n])

        @pl.when(ki < qi)
        def _():
            step(False)

        @pl.when(ki == qi)
        def _():
            step(True)
            dqn_ref[...] = dq_s[:, :HEAD_W].astype(dqn_ref.dtype)
            dqr_ref[...] = dq_s[:, HEAD_W:]

    qb = lambda p, s: _tri_q(p, s, nq)[0]
    kb = lambda p, s: _tri_q(p, s, nq)[1]
    qspec = pl.BlockSpec((tq, HEAD_W), lambda h, p, s: (qb(p, s), h))
    return pl.pallas_call(
        _with_rider(body, 7, 2, 2, n, scatter_ops, grid) if n else body, name="mla_dq", grid=grid,
        in_specs=[qspec, qspec,
                  pl.BlockSpec((tq, 2 * HEAD_W), lambda h, p, s: (kb(p, s), h)),
                  pl.BlockSpec((tq, HEAD_W), lambda h, p, s: (kb(p, s), 0)),
                  qspec, qspec,
                  pl.BlockSpec((None, tq, 1), lambda h, p, s: (h, qb(p, s), 0))] + [HBM] * n,
        out_specs=[qspec, qspec] + [HBM] * n,
        out_shape=[jax.ShapeDtypeStruct((t, N_HEADS * HEAD_W), BF16),
                   jax.ShapeDtypeStruct((t, N_HEADS * HEAD_W), F32)] + _exchange_shapes(ride, True),
        scratch_shapes=[pltpu.VMEM((tq, 2 * HEAD_W), F32), pltpu.VMEM((tq, 1), F32)]
        + (_exchange_sems(n) if n else []),
        compiler_params=_params(("arbitrary",) * 3 if n else ("parallel", "parallel", "arbitrary")),
    )(proj, qr, kv, kr, do, o, lse, *ride)


def _flash_dkv(proj, qr, kv, kr, do, o, lse):
    t = proj.shape[0]
    tq, nq, n_pairs, n_steps = _attn_tiles(t)

    def body(qn_ref, qr_ref, kv_ref, kr_ref, do_ref, o_ref, lse_ref, dkv_ref, dkr_ref, dk_s, dv_s):
        qi, ki = _tri_k(pl.program_id(1), pl.program_id(2), nq)

        @pl.when(qi == ki)
        def _():
            dk_s[...] = jnp.zeros_like(dk_s)
            dv_s[...] = jnp.zeros_like(dv_s)

        def step(diag):
            q, k = _qk(qn_ref, qr_ref, kv_ref, kr_ref)
            for r0, c in _row_chunks(tq):
                rows = slice(r0, r0 + c)
                n = r0 + c if diag else tq
                p = jnp.exp(_scores(q[rows], k[:n], r0, diag) - lse_ref[rows])
                do = do_ref[rows]
                dsum = jnp.sum(do * o_ref[rows], axis=1, keepdims=True)
                dv_s[:n] += _dot_tn(p, do)
                dp = _dot_nt(do, kv_ref[:n, HEAD_W:])
                ds = p * (dp - dsum) * SOFTMAX_SCALE
                dk_s[:n] += _dot_tn(ds, q[rows])

        @pl.when(qi == ki)
        def _():
            step(True)

        @pl.when(qi > ki)
        def _():
            step(False)

        @pl.when(qi == nq - 1)
        def _():
            dkv_ref[...] = jnp.concatenate([dk_s[:, :HEAD_W], dv_s[...]], axis=1).astype(dkv_ref.dtype)
            dkr_ref[...] = dk_s[:, HEAD_W:]

    qb = lambda p, s: _tri_k(p, s, nq)[0]
    kb = lambda p, s: _tri_k(p, s, nq)[1]
    qspec = pl.BlockSpec((tq, HEAD_W), lambda h, p, s: (qb(p, s), h))
    return pl.pallas_call(
        body, name="mla_dkv", grid=(N_HEADS, n_pairs, n_steps),
        in_specs=[qspec, qspec,
                  pl.BlockSpec((tq, 2 * HEAD_W), lambda h, p, s: (kb(p, s), h)),
                  pl.BlockSpec((tq, HEAD_W), lambda h, p, s: (kb(p, s), 0)),
                  qspec, qspec,
                  pl.BlockSpec((None, tq, 1), lambda h, p, s: (h, qb(p, s), 0))],
        out_specs=[pl.BlockSpec((tq, 2 * HEAD_W), lambda h, p, s: (kb(p, s), h)),
                   pl.BlockSpec((None, tq, HEAD_W), lambda h, p, s: (h, kb(p, s), 0))],
        out_shape=[jax.ShapeDtypeStruct((t, N_HEADS * 2 * HEAD_W), BF16),
                   jax.ShapeDtypeStruct((N_HEADS, t, HEAD_W), F32)],
        scratch_shapes=[pltpu.VMEM((tq, 2 * HEAD_W), F32), pltpu.VMEM((tq, HEAD_W), F32)],
        compiler_params=_params(("parallel", "parallel", "arbitrary")),
    )(proj, qr, kv, kr, do, o, lse)


def _block_diag(x, w):
    return jnp.concatenate(
        [_dot(x[:, b * LANES:(b + 1) * LANES], w[b]) for b in range(LRU_BLOCKS)], axis=1)


def _block_diag_t(d, w):
    return jnp.concatenate(
        [_dot_nt(d[:, b * LANES:(b + 1) * LANES], w[b]) for b in range(LRU_BLOCKS)], axis=1)


def _lru_gates(xc, w_rg, b_rg, w_ig, b_ig, lam):
    r = _sigmoid(_block_diag(xc, w_rg) + b_rg)
    ig = _sigmoid(_block_diag(xc, w_ig) + b_ig)
    sp = jnp.maximum(-lam, 0.0) + jnp.log1p(jnp.exp(-jnp.abs(lam)))
    log_a = -LRU_C * r * sp
    a = jnp.exp(log_a)
    mult = jnp.sqrt(-jnp.tanh(log_a) * (a * a + 1.0))
    return r, ig, sp, a, mult


def _lru_tm(t):
    return min(128, t)


def _halo_spec(tm, width, cb):
    return pl.BlockSpec((SUBLANES, width), lambda i, cb=cb: (jnp.maximum(i * (tm // SUBLANES) - 1, 0), cb))


def _lru_fwd(proj, w_conv, b_conv, w_rg, b_rg, w_ig, b_ig, lam):
    t = proj.shape[0]
    tm = _lru_tm(t)
    w = D_MODEL
    n_scan = int(math.log2(tm))

    def body(lx_ref, halo_ref, ly_ref, wc_ref, bc_ref, wrg_ref, brg_ref, wig_ref, big_ref, lam_ref,
             oli_ref, h_ref, carry_s):
        i = pl.program_id(0)

        @pl.when(i == 0)
        def _():
            carry_s[...] = jnp.zeros_like(carry_s)

        x = lx_ref[...]
        halo = jnp.where(i > 0, halo_ref[...], 0.0)
        xc = _causal_conv(_conv_taps(x, halo, 4), wc_ref[...], bc_ref[...])
        _, ig, _, a, mult = _lru_gates(xc, wrg_ref[...], brg_ref[...], wig_ref[...], big_ref[...], lam_ref[...])
        u = mult * (ig * xc)
        rows = _row_iota(a.shape)
        for s in range(n_scan):
            d = 1 << s
            keep = rows >= d
            a_s = jnp.where(keep, pltpu.roll(a, d, 0), 1.0)
            u_s = jnp.where(keep, pltpu.roll(u, d, 0), 0.0)
            u = a * u_s + u
            a = a * a_s
        h = u + a * carry_s[SUBLANES - 1:SUBLANES, :]
        h_ref[...] = h
        carry_s[...] = h[tm - SUBLANES:, :]
        oli_ref[...] = (h * _gelu(ly_ref[...])).astype(oli_ref.dtype)

    full = lambda arr: pl.BlockSpec(arr.shape, lambda i, nd=arr.ndim: (0,) * nd)
    return pl.pallas_call(
        body, name="lru_fwd", grid=(t // tm,),
        in_specs=[pl.BlockSpec((tm, w), lambda i: (i, SEG_LX)), _halo_spec(tm, w, SEG_LX),
                  pl.BlockSpec((tm, w), lambda i: (i, SEG_LY)),
                  full(w_conv), full(b_conv), full(w_rg), full(b_rg), full(w_ig), full(b_ig), full(lam)],
        out_specs=[pl.BlockSpec((tm, w), lambda i: (i, 0)), pl.BlockSpec((tm, w), lambda i: (i, 0))],
        out_shape=[jax.ShapeDtypeStruct((t, w), BF16), jax.ShapeDtypeStruct((t, w), F32)],
        scratch_shapes=[pltpu.VMEM((SUBLANES, w), F32)],
        compiler_params=_params(("arbitrary",)),
    )(proj, proj, proj, w_conv, b_conv, w_rg, b_rg, w_ig, b_ig, lam)


def _lru_bwd(proj, hl, doli, w_conv, b_conv, w_rg, b_rg, w_ig, b_ig, lam):
    t = proj.shape[0]
    tm = _lru_tm(t)
    nt = t // tm
    w = D_MODEL
    n_scan = int(math.log2(tm))

    def body(lx_ref, lxh_ref, ly_ref, h_ref, hh_ref, doli_ref,
             wc_ref, bc_ref, wrg_ref, brg_ref, wig_ref, big_ref, lam_ref,
             dlx_ref, dly_ref, dwc_ref, dbc_ref, dwrg_ref, dbrg_ref, dwig_ref, dbig_ref, dlam_ref,
             ca_s, cd_s, cx_s):
        i = pl.program_id(0)
        blk = nt - 1 - i

        @pl.when(i == 0)
        def _():
            for r in (ca_s, cd_s, cx_s, dwc_ref, dbc_ref, dwrg_ref, dbrg_ref, dwig_ref, dbig_ref, dlam_ref):
                r[...] = jnp.zeros_like(r)

        x = lx_ref[...]
        halo = jnp.where(blk > 0, lxh_ref[...], 0.0)
        taps = _conv_taps(x, halo, 4)
        wc = wc_ref[...]
        xc = _causal_conv(taps, wc, bc_ref[...])
        w_rg, w_ig, lam_v = wrg_ref[...], wig_ref[...], lam_ref[...]
        r, ig, sp, a, mult = _lru_gates(xc, w_rg, brg_ref[...], w_ig, big_ref[...], lam_v)
        h = h_ref[...]
        h_prev = _shift_down(h, 1, jnp.where(blk > 0, hh_ref[...], 0.0))
        gl, dgl = _gelu_and_grad(ly_ref[...])
        doli = doli_ref[...]
        dly_ref[...] = (doli * h * dgl).astype(dly_ref.dtype)
        acc_b = doli * gl
        acc_a = _shift_up(a, 1, ca_s[...])
        rows = _row_iota(a.shape)
        for s in range(n_scan):
            d = 1 << s
            keep = rows < tm - d
            a_s = jnp.where(keep, pltpu.roll(acc_a, tm - d, 0), 1.0)
            b_s = jnp.where(keep, pltpu.roll(acc_b, tm - d, 0), 0.0)
            acc_b = acc_b + acc_a * b_s
            acc_a = acc_a * a_s
        dht = acc_b + acc_a * cd_s[0:1, :]
        da = dht * h_prev
        dmult = dht * (ig * xc)
        di = dht * mult * xc
        dxc = dht * mult * ig
        dlog_a = da * a - dmult * (a * a) / mult
        dr = dlog_a * (-LRU_C * sp)
        dlam_ref[...] += jnp.sum(dlog_a * r, axis=0, keepdims=True) * (LRU_C * _sigmoid(-lam_v))
        dpr = dr * r * (1.0 - r)
        dpi = di * ig * (1.0 - ig)
        dxc = dxc + _block_diag_t(dpr, w_rg) + _block_diag_t(dpi, w_ig)
        dbrg_ref[...] += jnp.sum(dpr, axis=0, keepdims=True)
        dbig_ref[...] += jnp.sum(dpi, axis=0, keepdims=True)
        for b in range(LRU_BLOCKS):
            sl = slice(b * LANES, (b + 1) * LANES)
            dwrg_ref[b] += _dot_tn(xc[:, sl], dpr[:, sl])
            dwig_ref[b] += _dot_tn(xc[:, sl], dpi[:, sl])
        dbc_ref[...] += jnp.sum(dxc, axis=0, keepdims=True)
        dwc_ref[...] += jnp.concatenate(
            [jnp.sum(dxc * taps[3 - k], axis=0, keepdims=True) for k in range(4)], axis=0)
        cx = cx_s[...]
        dlx = wc[3:4] * dxc
        for s in range(1, 4):
            dlx = dlx + wc[3 - s:4 - s] * _shift_up(dxc, s, cx)
        dlx_ref[...] = dlx.astype(dlx_ref.dtype)
        ca_s[...] = a[:SUBLANES, :]
        cd_s[...] = dht[:SUBLANES, :]
        cx_s[...] = dxc[:SUBLANES, :]

    full = lambda arr: pl.BlockSpec(arr.shape, lambda i, nd=arr.ndim: (0,) * nd)
    rev = lambda cb: pl.BlockSpec((tm, w), lambda i, cb=cb: (nt - 1 - i, cb))
    halo = lambda cb: pl.BlockSpec(
        (SUBLANES, w), lambda i, cb=cb: (jnp.maximum((nt - 1 - i) * (tm // SUBLANES) - 1, 0), cb))
    acc = lambda s: pl.BlockSpec(s, lambda i, nd=len(s): (0,) * nd)
    acc_shapes = [(4, w), (1, w), (LRU_BLOCKS, LANES, LANES), (1, w), (LRU_BLOCKS, LANES, LANES), (1, w), (1, w)]
    return pl.pallas_call(
        body, name="lru_bwd", grid=(nt,),
        in_specs=[rev(SEG_LX), halo(SEG_LX), rev(SEG_LY), rev(0), halo(0), rev(0),
                  full(w_conv), full(b_conv), full(w_rg), full(b_rg), full(w_ig), full(b_ig), full(lam)],
        out_specs=[rev(0), rev(0)] + [acc(s) for s in acc_shapes],
        out_shape=[jax.ShapeDtypeStruct((t, w), BF16), jax.ShapeDtypeStruct((t, w), BF16)]
        + [jax.ShapeDtypeStruct(s, F32) for s in acc_shapes],
        scratch_shapes=[pltpu.VMEM((SUBLANES, w), F32)] * 3,
        compiler_params=_params(("arbitrary",)),
    )(proj, proj, proj, hl, hl, doli, w_conv, b_conv, w_rg, b_rg, w_ig, b_ig, lam)


FFN_TC = 512
FFN_NJ = D_FF // FFN_TC


def _ffn_tm(t):
    return min(512, t)


def _ffn_fwd(up_pre, w_fconv, b_fconv):
    t = up_pre.shape[0]
    tm = _ffn_tm(t)

    def body(g_ref, gh_ref, v_ref, vh_ref, wg_ref, wv_ref, bg_ref, bv_ref, act_ref):
        first = pl.program_id(0) > 0
        gate = _causal_conv(_conv_taps(g_ref[...], jnp.where(first, gh_ref[...], 0.0), 3), wg_ref[...], bg_ref[...])
        val = _causal_conv(_conv_taps(v_ref[...], jnp.where(first, vh_ref[...], 0.0), 3), wv_ref[...], bv_ref[...])
        act_ref[...] = (_gelu(gate) * val).astype(act_ref.dtype)

    blk = lambda off: pl.BlockSpec((tm, FFN_TC), lambda i, j: (i, j + off))
    halo = lambda off: pl.BlockSpec(
        (SUBLANES, FFN_TC), lambda i, j: (jnp.maximum(i * (tm // SUBLANES) - 1, 0), j + off))
    wsp = lambda rows, off: pl.BlockSpec((rows, FFN_TC), lambda i, j: (0, j + off))
    return pl.pallas_call(
        body, name="ffn_conv_fwd", grid=(t // tm, FFN_NJ),
        in_specs=[blk(0), halo(0), blk(FFN_NJ), halo(FFN_NJ), wsp(3, 0), wsp(3, FFN_NJ), wsp(1, 0), wsp(1, FFN_NJ)],
        out_specs=pl.BlockSpec((tm, FFN_TC), lambda i, j: (i, j)),
        out_shape=jax.ShapeDtypeStruct((t, D_FF), BF16),
        compiler_params=_params(("parallel", "parallel")),
    )(up_pre, up_pre, up_pre, up_pre, w_fconv, w_fconv, b_fconv, b_fconv)


def _ffn_bwd_act(up_pre, dact, w_fconv, b_fconv):
    t = up_pre.shape[0]
    tm = _ffn_tm(t)

    def body(g_ref, gh_ref, v_ref, vh_ref, da_ref, wg_ref, wv_ref, bg_ref, bv_ref, dup_ref, dw_ref, db_ref):
        jj, i = pl.program_id(0), pl.program_id(1)

        @pl.when(i == 0)
        def _():
            dw_ref[...] = jnp.zeros_like(dw_ref)
            db_ref[...] = jnp.zeros_like(db_ref)

        g_taps = _conv_taps(g_ref[...], jnp.where(i > 0, gh_ref[...], 0.0), 3)
        v_taps = _conv_taps(v_ref[...], jnp.where(i > 0, vh_ref[...], 0.0), 3)
        gate = _causal_conv(g_taps, wg_ref[...], bg_ref[...])
        val = _causal_conv(v_taps, wv_ref[...], bv_ref[...])
        gl, dgl = _gelu_and_grad(gate)
        dact = da_ref[...]
        is_gate = jj < FFN_NJ
        d = jnp.where(is_gate, dact * val * dgl, dact * gl)
        dup_ref[...] = d
        db_ref[...] += jnp.sum(d, axis=0, keepdims=True)
        dw_ref[...] += jnp.concatenate(
            [jnp.sum(d * jnp.where(is_gate, g_taps[2 - k], v_taps[2 - k]), axis=0, keepdims=True)
             for k in range(3)], axis=0)

    jm = lambda jj: jj % FFN_NJ
    blk = lambda off: pl.BlockSpec((tm, FFN_TC), lambda jj, i: (i, jm(jj) + off))
    halo = lambda off: pl.BlockSpec(
        (SUBLANES, FFN_TC), lambda jj, i: (jnp.maximum(i * (tm // SUBLANES) - 1, 0), jm(jj) + off))
    wsp = lambda rows, off: pl.BlockSpec((rows, FFN_TC), lambda jj, i: (0, jm(jj) + off))
    return pl.pallas_call(
        body, name="ffn_bwd_act", grid=(2 * FFN_NJ, t // tm),
        in_specs=[blk(0), halo(0), blk(FFN_NJ), halo(FFN_NJ), blk(0),
                  wsp(3, 0), wsp(3, FFN_NJ), wsp(1, 0), wsp(1, FFN_NJ)],
        out_specs=[pl.BlockSpec((tm, FFN_TC), lambda jj, i: (i, jj)),
                   pl.BlockSpec((3, FFN_TC), lambda jj, i: (0, jj)),
                   pl.BlockSpec((1, FFN_TC), lambda jj, i: (0, jj))],
        out_shape=[jax.ShapeDtypeStruct((t, 2 * D_FF), F32), jax.ShapeDtypeStruct((3, 2 * D_FF), F32),
                   jax.ShapeDtypeStruct((1, 2 * D_FF), F32)],
        compiler_params=_params(("parallel", "arbitrary")),
    )(up_pre, up_pre, up_pre, up_pre, dact, w_fconv, w_fconv, b_fconv, b_fconv)


def _ffn_bwd_conv(dup, w_fconv):
    t = dup.shape[0]
    tm = _ffn_tm(t)
    nt = t // tm

    def body(d_ref, dh_ref, w_ref, out_ref):
        d = d_ref[...]
        halo = jnp.where(pl.program_id(0) < nt - 1, dh_ref[...], 0.0)
        wv = w_ref[...]
        out = wv[2:3] * d + wv[1:2] * _shift_up(d, 1, halo) + wv[0:1] * _shift_up(d, 2, halo)
        out_ref[...] = out.astype(out_ref.dtype)

    return pl.pallas_call(
        body, name="ffn_bwd_conv", grid=(nt, 2 * FFN_NJ),
        in_specs=[pl.BlockSpec((tm, FFN_TC), lambda i, j: (i, j)),
                  pl.BlockSpec((SUBLANES, FFN_TC),
                               lambda i, j: (jnp.minimum((i + 1) * (tm // SUBLANES), t // SUBLANES - 1), j)),
                  pl.BlockSpec((3, FFN_TC), lambda i, j: (0, j))],
        out_specs=pl.BlockSpec((tm, FFN_TC), lambda i, j: (i, j)),
        out_shape=jax.ShapeDtypeStruct((t, 2 * D_FF), BF16),
        compiler_params=_params(("parallel", "parallel")),
    )(dup, dup, w_fconv)


def _xattn_probs(cq, ck, h):
    sl = slice(h * LANES, (h + 1) * LANES)
    s = _dot_nt(cq[:, sl], ck[:, sl]) * X_SCALE
    e = jnp.exp(s - jnp.max(s, axis=1, keepdims=True))
    return e / jnp.sum(e, axis=1, keepdims=True), sl


def _xattn_fwd_fn(cq, ck, cv):
    outs = []
    for h in range(X_HEADS):
        p, sl = _xattn_probs(cq, ck, h)
        outs.append(_dot(p, cv[:, sl]))
    return (jnp.concatenate(outs, axis=1),)


def _xattn_bwd_fn(cq, dco, ck, cv):
    dcq, dck, dcv = [], [], []
    for h in range(X_HEADS):
        p, sl = _xattn_probs(cq, ck, h)
        dcv.append(_dot_tn(p, dco[:, sl]))
        dp = _dot_nt(dco[:, sl], cv[:, sl])
        ds = p * (dp - jnp.sum(p * dp, axis=1, keepdims=True)) * X_SCALE
        dcq.append(_dot(ds, ck[:, sl]))
        dck.append(_dot_tn(ds, cq[:, sl]))
    return jnp.concatenate(dcq, axis=1), jnp.concatenate(dck, axis=1), jnp.concatenate(dcv, axis=1)


def _perm_w_in(w):
    q = w[:, :3072].reshape(D_MODEL, N_HEADS, 192)
    qn = q[:, :, :128].reshape(D_MODEL, 2048)
    qr = jnp.pad(q[:, :, 128:], ((0, 0), (0, 0), (0, 64))).reshape(D_MODEL, 2048)
    ckv = w[:, 3072:3584]
    kr = jnp.pad(w[:, 3584:3648], ((0, 0), (0, 64)))
    rest = w[:, 3648:]
    pad = jnp.zeros((D_MODEL, PROJ_W - 12928), w.dtype)
    return jnp.concatenate([qn, qr, rest, ckv, kr, pad], axis=1)


def _unperm_dw_in(dw):
    qn = dw[:, :2048].reshape(D_MODEL, N_HEADS, 128)
    qr = dw[:, 2048:4096].reshape(D_MODEL, N_HEADS, 128)[:, :, :64]
    q = jnp.concatenate([qn, qr], axis=2).reshape(D_MODEL, 3072)
    return jnp.concatenate([q, dw[:, 12288:12800], dw[:, 12800:12864], dw[:, 4096:12288]], axis=1)


def _whole(defs, gathered):
    return {n: g.reshape(-1, g.shape[-1]) if ax == 0 else jnp.concatenate([g[j] for j in range(4)], axis=1)
            for (n, _, ax), g in zip(defs, gathered)}


def _local_step(x, mem, positions, tgt, wb, ws, late_weights=(), early_grads=()):
    t = x.shape[0]
    n_mem = mem.shape[0]
    tm = min(256, t)
    row = lambda v: v.reshape(1, -1)
    g_pre_mix, g_post_mix, g_ckv = row(ws["g_pre_mix"]), row(ws["g_post_mix"]), row(ws["g_ckv"])
    g_pre_x, g_post_x, g_mem = row(ws["g_pre_x"]), row(ws["g_post_x"]), row(ws["g_mem"])
    g_pre_ffn, g_post_ffn = row(ws["g_pre_ffn"]), row(ws["g_post_ffn"])
    b_conv, lam, b_fconv = row(ws["b_conv_lru"]), row(ws["lru_lambda"]), row(ws["b_fconv"])
    b_rg, b_ig = row(ws["b_rg"]), row(ws["b_ig"])
    w_rg, w_ig = ws["w_rg"].astype(BF16), ws["w_ig"].astype(BF16)
    w_in = _perm_w_in(wb["w_in"])
    W = D_MODEL

    def dw(name, a, b, owners=1, tn=1024):
        g = _mm(name, a, b, "tn", out_dtype=BF16, owners=owners, tn=tn)
        return g.reshape(8, -1, g.shape[-1])

    cos_t, s1_t, s2_t = _rope_tables(positions.reshape(t, 1), tm)
    (h1,) = _rows("rms_pre_mix", lambda x, g: (_rms(x, g),), t, tm, [(x, W, 0)], [g_pre_mix], [(W, BF16)])
    proj = _mm("proj", h1, w_in, "nn")

    def rope_ckv_fn(qr, ckv, kr, c, s1, s2, g):
        return _rope_heads(qr, c, s1, s2), _rope_group(kr, c, s1, s2), _rms(ckv, g)

    qr, kr, ckvn = _rows("rope_ckv", rope_ckv_fn, t, tm,
                         [(proj, W, SEG_QR), (proj, 512, CKV_BLOCK), (proj, LANES, KR_BLOCK),
                          (cos_t, LANES, 0), (s1_t, LANES, 0), (s2_t, LANES, 0)], [g_ckv],
                         [(W, BF16), (LANES, BF16), (512, BF16)])
    kv = _mm("kv_up", ckvn, wb["w_ukv"], "nn", out_dtype=BF16)
    o, lse, *gathered = _flash_fwd(proj, qr, kv, kr, ride=[blk for _, blk in late_weights])
    wb = {**wb, **_whole([d for d, _ in late_weights], gathered)}
    w_conv, w_fconv = wb["w_conv_lru"], wb["w_fconv"]
    o_mla = _mm("o_mla", o, wb["w_o_mla"], "nn")
    oli, hl = _lru_fwd(proj, w_conv, b_conv, w_rg, b_rg, w_ig, b_ig, lam)
    o_lru = _mm("o_lru", oli, wb["w_o_lru"], "nn")
    (merged,) = _rows("merge", lambda gm, gl, a, b: (_sigmoid(gm) * a + _sigmoid(gl) * b,), t, tm,
                      [(proj, W, SEG_GM), (proj, W, SEG_GL), (o_mla, W, 0), (o_lru, W, 0)], [], [(W, BF16)])
    z1 = _mm("w_out", merged, wb["w_out"], "nn")

    def post_fn(x, z, g_post, g_pre):
        x1 = x + _rms(z, g_post)
        return x1, _rms(x1, g_pre)

    x1, h2 = _rows("post_mix", post_fn, t, tm, [(x, W, 0), (z1, W, 0)], [g_post_mix, g_pre_x],
                   [(W, F32), (W, BF16)])
    cq = _mm("cq", h2, wb["w_cq"], "nn", out_dtype=BF16)
    (mn,) = _rows("rms_mem", lambda m, g: (_rms(m, g),), n_mem, n_mem, [(mem, W, 0)], [g_mem], [(W, BF16)])
    ck = _mm("ck", mn, wb["w_ck"], "nn", out_dtype=BF16)
    cv = _mm("cv", mn, wb["w_cv"], "nn", out_dtype=BF16)
    (co,) = _rows("xattn_fwd", _xattn_fwd_fn, t, tm, [(cq, 512, 0)], [ck, cv], [(512, BF16)])
    z2 = _mm("w_co", co, wb["w_co"], "nn")
    x2, h3 = _rows("post_x", post_fn, t, tm, [(x1, W, 0), (z2, W, 0)], [g_post_x, g_pre_ffn],
                   [(W, F32), (W, BF16)])
    up_pre = _mm("w_up", h3, wb["w_up"], "nn")
    act = _ffn_fwd(up_pre, w_fconv, b_fconv)
    z3 = _mm("w_down", act, wb["w_down"], "nn")

    def loss_fn(x2, z3, tgt, g):
        err = x2 + _rms(z3, g) - tgt
        dy = err * (1.0 / W)
        dz, dg = _rms_bwd(z3, g, dy)
        part = 0.5 * jnp.sum(err * err) * (1.0 / W)
        return dy, dz, jnp.zeros((SUBLANES, LANES), F32) + part, dg

    dy, dz3, loss_acc, dg_post_ffn = _rows(
        "loss", loss_fn, t, tm, [(x2, W, 0), (z3, W, 0), (tgt, W, 0)], [g_post_ffn],
        [(W, F32), (W, BF16)], [(SUBLANES, LANES), (1, W)])
    grads = {"g_post_ffn": dg_post_ffn}
    dact = _mm("d_act", dz3, wb["w_down"], "nt")
    grads["w_down"] = dw("dw_down", act, dz3)
    dup, grads["w_fconv"], grads["b_fconv"] = _ffn_bwd_act(up_pre, dact, w_fconv, b_fconv)
    dup_pre = _ffn_bwd_conv(dup, w_fconv)
    dh3 = _mm("d_h3", dup_pre, wb["w_up"], "nt")
    grads["w_up"] = dw("dw_up", h3, dup_pre, owners=4, tn=1408)

    def res_bwd_fn(dres, xa, dh, z, g_pre, g_post):
        dxa, dg_pre = _rms_bwd(xa, g_pre, dh)
        dxa = dres + dxa
        dz, dg_post = _rms_bwd(z, g_post, dxa)
        return dxa, dz, dg_pre, dg_post

    dx2, dz2, grads["g_pre_ffn"], grads["g_post_x"] = _rows(
        "bwd_post_x", res_bwd_fn, t, tm, [(dy, W, 0), (x2, W, 0), (dh3, W, 0), (z2, W, 0)],
        [g_pre_ffn, g_post_x], [(W, F32), (W, BF16)], [(1, W), (1, W)])
    dco = _mm("d_co", dz2, wb["w_co"], "nt")
    grads["w_co"] = dw("dw_co", co, dz2, owners=4)
    dcq, dck, dcv = _rows("xattn_bwd", _xattn_bwd_fn, t, tm, [(cq, 512, 0), (dco, 512, 0)], [ck, cv],
                          [(512, BF16)], [(n_mem, 512), (n_mem, 512)])
    dh2 = _mm("d_h2", dcq, wb["w_cq"], "nt")
    grads["w_cq"] = dw("dw_cq", h2, dcq)
    grads["w_ck"] = dw("dw_ck", mn, dck)
    grads["w_cv"] = dw("dw_cv", mn, dcv)
    dmn_k = _mm("d_mn_k", dck, wb["w_ck"], "nt")
    dmn_v = _mm("d_mn_v", dcv, wb["w_cv"], "nt")
    (grads["g_mem"],) = _rows("dg_mem", lambda m, a, b: (jnp.sum((a + b) * m * _rsq(m), axis=0, keepdims=True),),
                              n_mem, n_mem, [(mem, W, 0), (dmn_k, W, 0), (dmn_v, W, 0)], [], [], [(1, W)])
    dx1, dz1, grads["g_pre_x"], grads["g_post_mix"] = _rows(
        "bwd_post_mix", res_bwd_fn, t, tm, [(dx2, W, 0), (x1, W, 0), (dh2, W, 0), (z1, W, 0)],
        [g_pre_x, g_post_mix], [(W, F32), (W, BF16)], [(1, W), (1, W)])
    dmerged = _mm("d_merged", dz1, wb["w_out"], "nt")
    grads["w_out"] = dw("dw_out", merged, dz1)

    def merge_bwd_fn(dm, gm, gl, a, b):
        sm, sl = _sigmoid(gm), _sigmoid(gl)
        return dm * sm, dm * sl, dm * a * sm * (1.0 - sm), dm * b * sl * (1.0 - sl)

    do_mla, do_lru, dgm, dgl = _rows(
        "merge_bwd", merge_bwd_fn, t, tm,
        [(dmerged, W, 0), (proj, W, SEG_GM), (proj, W, SEG_GL), (o_mla, W, 0), (o_lru, W, 0)], [],
        [(W, BF16)] * 4)
    do = _mm("d_o", do_mla, wb["w_o_mla"], "nt")
    grads["w_o_mla"] = dw("dw_o_mla", o, do_mla)
    dqn, dqr_pre, *landed = _flash_dq(proj, qr, kv, kr, do, o, lse, ride=[grads[n] for n in early_grads])
    grads.update(zip(early_grads, landed))
    dkv, dkr_h = _flash_dkv(proj, qr, kv, kr, do, o, lse)
    dckvn = _mm("d_ckvn", dkv, wb["w_ukv"], "nt")
    grads["w_ukv"] = dw("dw_ukv", ckvn, dkv, owners=4)

    def rope_bwd_fn(dqr, dckvn, ckv, c, s1, s2, dkr_h, g):
        dkr = dkr_h[0]
        for h in range(1, N_HEADS):
            dkr = dkr + dkr_h[h]
        dckv, dg = _rms_bwd(ckv, g, dckvn)
        return _rope_heads(dqr, c, -s1, -s2), _rope_group(dkr, c, -s1, -s2), dckv, dg

    dqr, dkr, dckv, grads["g_ckv"] = _rope_bwd(rope_bwd_fn, t, tm, dqr_pre, dckvn, proj, cos_t, s1_t, s2_t, dkr_h, g_ckv)
    doli = _mm("d_oli", do_lru, wb["w_o_lru"], "nt")
    grads["w_o_lru"] = dw("dw_o_lru", oli, do_lru)
    (dlx, dly, grads["w_conv_lru"], grads["b_conv_lru"], grads["w_rg"], grads["b_rg"], grads["w_ig"],
     grads["b_ig"], grads["lru_lambda"]) = _lru_bwd(proj, hl, doli, w_conv, b_conv, w_rg, b_rg, w_ig, b_ig, lam)
    dproj = jnp.concatenate([dqn, dqr, dlx, dly, dgm, dgl, dckv, dkr,
                             jnp.zeros((t, PROJ_W - 12928), BF16)], axis=1)
    dh1 = _mm("d_h1", dproj, w_in, "nt")
    dw_in = _unperm_dw_in(_mm("dw_in", h1, dproj, "tn", out_dtype=BF16))
    grads["w_in"] = dw_in.reshape(W, 4, -1).transpose(1, 0, 2).reshape(8, W // 2, -1)

    def in_bwd_fn(dres, x, dh, g):
        dx, dg = _rms_bwd(x, g, dh)
        return dres + dx, dg

    grad_x, grads["g_pre_mix"] = _rows("bwd_pre_mix", in_bwd_fn, t, tm, [(dx1, W, 0), (x, W, 0), (dh1, W, 0)],
                                       [g_pre_mix], [(W, F32)], [(1, W)])
    return loss_acc[0, 0], grad_x, grads


def _rope_bwd(fn, t, tm, dqr_pre, dckvn, proj, cos_t, s1_t, s2_t, dkr_h, g_ckv):
    W = D_MODEL

    def body(dqr_ref, dck_ref, ckv_ref, c_ref, s1_ref, s2_ref, dkrh_ref, g_ref, o1, o2, o3, o4):
        r1, r2, r3, dg = fn(dqr_ref[...], dck_ref[...], ckv_ref[...], c_ref[...], s1_ref[...], s2_ref[...],
                            dkrh_ref[...], g_ref[...])
        o1[...] = r1.astype(o1.dtype)
        o2[...] = r2.astype(o2.dtype)
        o3[...] = r3.astype(o3.dtype)

        @pl.when(pl.program_id(0) == 0)
        def _():
            o4[...] = jnp.zeros_like(o4)

        o4[...] += dg

    rb = lambda w, cb=0: pl.BlockSpec((tm, w), lambda i, cb=cb: (i, cb))
    return pl.pallas_call(
        body, name="rope_bwd", grid=(t // tm,),
        in_specs=[rb(W), rb(512), rb(512, CKV_BLOCK), rb(LANES), rb(LANES), rb(LANES),
                  pl.BlockSpec((N_HEADS, tm, LANES), lambda i: (0, i, 0)),
                  pl.BlockSpec((1, 512), lambda i: (0, 0))],
        out_specs=[rb(W), rb(LANES), rb(512), pl.BlockSpec((1, 512), lambda i: (0, 0))],
        out_shape=[jax.ShapeDtypeStruct((t, W), BF16), jax.ShapeDtypeStruct((t, LANES), BF16),
                   jax.ShapeDtypeStruct((t, 512), BF16), jax.ShapeDtypeStruct((1, 512), F32)],
        compiler_params=_params(("arbitrary",)),
    )(dqr_pre, dckvn, proj, cos_t, s1_t, s2_t, dkr_h, g_ckv)


HBM = pl.BlockSpec(memory_space=pl.ANY)


def _me():
    return lax.axis_index("x"), lax.axis_index("y"), lax.axis_index("c")


def _gather_chips_ops(srcs, outs, send_sems, recv_sems, local_sems):
    n = len(srcs)
    x, y, c = _me()
    chips = [(1 - x, y), (x, 1 - y), (1 - x, 1 - y)]

    def copy(i, k, slot, to):
        return pltpu.make_async_remote_copy(
            src_ref=srcs[i], dst_ref=outs[i].at[slot], send_sem=send_sems.at[i, k], recv_sem=recv_sems.at[i, k],
            device_id=to, device_id_type=MESH)

    local = [pltpu.make_async_copy(srcs[i], outs[i].at[2 * x + y], local_sems.at[i]) for i in range(n)]
    sends = [copy(i, k, 2 * x + y, (px, py, c)) for i in range(n) for k, (px, py) in enumerate(chips)]

    def start():
        for cp in local + sends:
            cp.start()

    def finish():
        for i in range(n):
            for k, (px, py) in enumerate(chips):
                copy(i, k, 2 * px + py, (px, py, c)).wait_recv()
        for cp in sends:
            cp.wait_send()
        for cp in local:
            cp.wait()

    return start, finish


def _gather_sems(n):
    return [pltpu.SemaphoreType.DMA((n, 3)), pltpu.SemaphoreType.DMA((n, 3)), pltpu.SemaphoreType.DMA((n,))]


def _gather_chips(arrs):
    n = len(arrs)

    def body(*refs):
        start, finish = _gather_chips_ops(refs[:n], refs[n:2 * n], *refs[2 * n:])
        start()
        finish()

    return pl.pallas_call(
        body, name="gather_weights", in_specs=[HBM] * n, out_specs=[HBM] * n,
        out_shape=[jax.ShapeDtypeStruct((4,) + a.shape, a.dtype) for a in arrs],
        scratch_shapes=_gather_sems(n),
    )(*arrs)


def _exchange_all(name, arrs, scatter):
    n = len(arrs)

    def body(*refs):
        start, finish = _exchange_ops(refs[:n], refs[n:2 * n], *refs[2 * n:], scatter=scatter)
        start()
        finish()

    return pl.pallas_call(
        body, name=name, in_specs=[HBM] * n, out_specs=[HBM] * n,
        out_shape=_exchange_shapes(arrs, scatter), scratch_shapes=_exchange_sems(n),
    )(*arrs)


def _exchange_shapes(arrs, scatter):
    return [jax.ShapeDtypeStruct((8,) + (a.shape[1:] if scatter else a.shape), a.dtype) for a in arrs]


def _exchange_sems(n):
    return [pltpu.SemaphoreType.DMA((n, 8)), pltpu.SemaphoreType.DMA((n, 8)), pltpu.SemaphoreType.DMA((n,))]


def _exchange_ops(srcs, outs, send_sems, recv_sems, local_sems, scatter):
    n = len(srcs)
    x, y, c = _me()
    me = 4 * x + 2 * y + c

    def copy(i, q):
        return pltpu.make_async_remote_copy(
            src_ref=srcs[i].at[q] if scatter else srcs[i], dst_ref=outs[i].at[me],
            send_sem=send_sems.at[i, q], recv_sem=recv_sems.at[i, me],
            device_id=(q // 4, (q // 2) % 2, q % 2), device_id_type=MESH)

    def arrival(i, s):
        return pltpu.make_async_remote_copy(
            src_ref=srcs[i].at[s] if scatter else srcs[i], dst_ref=outs[i].at[s],
            send_sem=send_sems.at[i, s], recv_sem=recv_sems.at[i, s],
            device_id=(s // 4, (s // 2) % 2, s % 2), device_id_type=MESH)

    local = [pltpu.make_async_copy(srcs[i].at[me] if scatter else srcs[i], outs[i].at[me], local_sems.at[i])
             for i in range(n)]

    def start():
        for cp in local:
            cp.start()
        for q in range(8):
            @pl.when(me != q)
            def _(q=q):
                for i in range(n):
                    copy(i, q).start()

    def finish():
        for s in range(8):
            @pl.when(me != s)
            def _(s=s):
                for i in range(n):
                    arrival(i, s).wait_recv()
        for q in range(8):
            @pl.when(me != q)
            def _(q=q):
                for i in range(n):
                    copy(i, q).wait_send()
        for cp in local:
            cp.wait()

    return start, finish


D2D_CHUNK_BYTES = 2 * 1024 * 1024


def _gather_cores(arrs):
    n = len(arrs)
    chunks = []
    for i, a in enumerate(arrs):
        r = a.shape[0]
        k = 1
        while a.size * a.dtype.itemsize // k > D2D_CHUNK_BYTES and r % (2 * k) == 0 and (r // (2 * k)) % SUBLANES == 0:
            k *= 2
        chunks += [(i, j * (r // k), r // k) for j in range(k)]
    nc = len(chunks)

    def body(*refs):
        srcs, outs = refs[:n], refs[n:2 * n]
        send_sems, recv_sems, local_sems = refs[2 * n:]
        x, y, c = _me()

        def copy(j, slot):
            i, r0, nr = chunks[j]
            return pltpu.make_async_remote_copy(
                src_ref=srcs[i].at[pl.ds(r0, nr)], dst_ref=outs[i].at[slot, pl.ds(r0, nr)],
                send_sem=send_sems.at[j], recv_sem=recv_sems.at[j],
                device_id=(x, y, 1 - c), device_id_type=MESH)

        local = [pltpu.make_async_copy(srcs[i].at[pl.ds(r0, nr)], outs[i].at[c, pl.ds(r0, nr)], local_sems.at[j])
                 for j, (i, r0, nr) in enumerate(chunks)]
        sends = [copy(j, c) for j in range(nc)]
        for cp in sends:
            cp.start()
        for cp in local:
            cp.start()
        for j in range(nc):
            copy(j, 1 - c).wait_recv()
        for cp in sends:
            cp.wait_send()
        for cp in local:
            cp.wait()

    return pl.pallas_call(
        body, name="gather_cores", in_specs=[HBM] * n, out_specs=[HBM] * n,
        out_shape=[jax.ShapeDtypeStruct((2,) + a.shape, a.dtype) for a in arrs],
        scratch_shapes=[pltpu.SemaphoreType.DMA((nc,))] * 3,
    )(*arrs)


def _row_tile(rows, row_bytes, budget):
    for t in (2048, 1024, 512, 256, 128, 64, 32, 16, 8):
        if rows % t == 0 and t * row_bytes <= budget:
            return t
    return rows


def _sum_slots(name, buf):
    _, r, w = buf.shape
    tr = _row_tile(r, 8 * w * buf.dtype.itemsize, 8 * 1024 * 1024)

    def body(b_ref, o_ref):
        acc = b_ref[0].astype(F32)
        for s in range(1, 8):
            acc = acc + b_ref[s].astype(F32)
        o_ref[...] = acc

    return pl.pallas_call(
        body, name=name, grid=(r // tr,),
        in_specs=[pl.BlockSpec((8, tr, w), lambda i: (0, i, 0))],
        out_specs=pl.BlockSpec((tr, w), lambda i: (i, 0)),
        out_shape=jax.ShapeDtypeStruct((r, w), F32),
        compiler_params=_params(("parallel",)),
    )(buf)


def _adamw(name, w, g, m, v, tr):
    def fn(w, g, m, v):
        m = ADAM_B1 * m + (1.0 - ADAM_B1) * g
        v = ADAM_B2 * v + (1.0 - ADAM_B2) * (g * g)
        m_hat = m / (1.0 - ADAM_B1 ** ADAM_STEP)
        v_hat = v / (1.0 - ADAM_B2 ** ADAM_STEP)
        delta = -ADAM_LR * (m_hat / (jnp.sqrt(v_hat) + ADAM_EPS) + ADAM_WD * w)
        return delta, m, v

    r, c = w.shape
    return _rows(name, fn, r, tr, [(a, c, 0) for a in (w, g, m, v)], [], [(c, F32)] * 3)


def _pack(arrays, n_rows, dtype):
    flat = jnp.concatenate([a.reshape(-1).astype(dtype) for a in arrays])
    return jnp.pad(flat, (0, n_rows * FLAT_W - flat.shape[0])).reshape(n_rows, FLAT_W)


def _unpack(flat, shapes):
    flat = flat.reshape(-1)
    out, off = [], 0
    for s in shapes:
        n = math.prod(s)
        out.append(flat[off:off + n].reshape(s))
        off += n
    return out


def kernel(x, mem, positions, g_pre_mix, g_post_mix, w_in, g_ckv, w_ukv, w_o_mla, w_conv_lru, b_conv_lru, w_rg, b_rg, w_ig, b_ig, lru_lambda, w_o_lru, w_out, g_pre_x, g_post_x, g_mem, w_cq, w_ck, w_cv, w_co, g_pre_ffn, g_post_ffn, w_up, w_fconv, b_fconv, w_down, loss_target, m_g_pre_mix, m_g_post_mix, m_w_in, m_g_ckv, m_w_ukv, m_w_o_mla, m_w_conv_lru, m_b_conv_lru, m_w_rg, m_b_rg, m_w_ig, m_b_ig, m_lru_lambda, m_w_o_lru, m_w_out, m_g_pre_x, m_g_post_x, m_g_mem, m_w_cq, m_w_ck, m_w_cv, m_w_co, m_g_pre_ffn, m_g_post_ffn, m_w_up, m_w_fconv, m_b_fconv, m_w_down, v_g_pre_mix, v_g_post_mix, v_w_in, v_g_ckv, v_w_ukv, v_w_o_mla, v_w_conv_lru, v_b_conv_lru, v_w_rg, v_b_rg, v_w_ig, v_b_ig, v_lru_lambda, v_w_o_lru, v_w_out, v_g_pre_x, v_g_post_x, v_g_mem, v_w_cq, v_w_ck, v_w_cv, v_w_co, v_g_pre_ffn, v_g_post_ffn, v_w_up, v_w_fconv, v_b_fconv, v_w_down):
    given = dict(locals())
    small_names = [n for n, _ in SMALL]
    conv_names = [n for n, _, _ in CONV]

    mine = {n: given[n][0].astype(BF16) for n, _, _ in BIG}
    mine.update({n: given[n][0] for n in conv_names})
    first = [d for d in BIG if d[0] in FIRST_WEIGHTS]
    late = [(d, mine[d[0]]) for d in BIG + CONV if d[0] not in FIRST_WEIGHTS]
    wb = _whole(first, _gather_chips([mine[n] for n, _, _ in first]))
    ws = {n: given[n][0] for n in small_names}

    loss_part, grad_x, grads = _local_step(x[0], mem[0], positions[0], loss_target[0], wb, ws,
                                           late_weights=late, early_grads=EARLY_GRADS)
    loss = lax.psum(loss_part, ("x", "y", "c"))

    late_names = [n for n, _, _ in BIG if n not in EARLY_GRADS]
    grads.update(zip(late_names, _exchange_all("scatter_grads", [grads[n] for n in late_names], scatter=True)))
    pieces = [_sum_slots("reduce_" + n, grads[n]) for n, _, _ in BIG]
    g_big = {n: g.reshape(s) for (n, s, _), g in zip(BIG, _gather_cores(pieces))}

    xs_shapes = [s for _, s in SMALL] + [(s[0], 4 * s[1]) for _, s, _ in CONV]
    s_part = _pack([grads[n].reshape(s) for n, s in zip(small_names + conv_names, xs_shapes)], SMALL_XCHG_ROWS, F32)
    (s_all,) = _exchange_all("gather_small", [s_part], scatter=False)
    g_small = dict(zip(small_names + conv_names, _unpack(_sum_slots("reduce_small", s_all), xs_shapes)))
    chip = 2 * lax.axis_index("x") + lax.axis_index("y")
    for n, s, _ in CONV:
        g_small[n] = lax.dynamic_slice_in_dim(g_small[n], chip * s[1], s[1], axis=1)

    out = {n: (g_big[n],) + tuple(_adamw("adamw_" + n, given[n][0], g_big[n], given["m_" + n][0], given["v_" + n][0],
                                         _row_tile(s[0], 4 * s[1], 1536 * 1024))) for n, s, _ in BIG}
    sm_shapes = [s for _, s in SMALL] + [s for _, s, _ in CONV]
    pack_small = lambda pre: _pack([given[pre + n][0] for n in small_names + conv_names], SMALL_ROWS, F32)
    g_flat = _pack([g_small[n] for n in small_names + conv_names], SMALL_ROWS, F32)
    small_out = (g_flat,) + tuple(_adamw("adamw_small", pack_small(""), g_flat, pack_small("m_"), pack_small("v_"), 280))
    for kind in range(4):
        for n, a in zip(small_names + conv_names, _unpack(small_out[kind], sm_shapes)):
            out.setdefault(n, [None] * 4)
            out[n] = list(out[n])
            out[n][kind] = a
    outs = [out[n][kind][None] for kind in range(4) for n in WEIGHTS]
    return (loss, grad_x[None], *outs)
```

```python
import functools
import math

import jax
import jax.numpy as jnp
from jax import lax
from jax.experimental import pallas as pl
from jax.experimental.pallas import tpu as pltpu

F32 = jnp.float32
BF16 = jnp.bfloat16
MESH = pl.DeviceIdType.MESH

EPS = 1e-6
D_MODEL = 2048
N_HEADS = 16
HEAD_W = 128
ROPE_HALF = 32
SOFTMAX_SCALE = 192 ** -0.5
ROPE_THETA = 10000.0
LRU_BLOCKS = 16
LRU_C = 8.0
X_HEADS = 4
X_SCALE = 128 ** -0.5
D_FF = 5632

ADAM_LR = 0.001
ADAM_B1 = 0.9
ADAM_B2 = 0.999
ADAM_EPS = 1e-08
ADAM_WD = 0.01
ADAM_STEP = 10

V7X_VMEM_LIMIT = 56 * 1024 * 1024
LANES = 128
SUBLANES = 8
ATTN_TILE = 1024
ATTN_CHUNK = 256
MM_BLOCK_BYTES = 4 * 1024 * 1024

PROJ_W = 13312
SEG_QN, SEG_QR, SEG_LX, SEG_LY, SEG_GM, SEG_GL = range(6)
CKV_BLOCK = 24
KR_BLOCK = 100

BIG = (
    ("w_in", (2048, 2960), 1), ("w_ukv", (512, 1024), 1), ("w_o_mla", (512, 2048), 0),
    ("w_o_lru", (512, 2048), 0), ("w_out", (512, 2048), 0),
    ("w_cq", (512, 512), 0), ("w_ck", (512, 512), 0), ("w_cv", (512, 512), 0),
    ("w_co", (512, 512), 1), ("w_up", (2048, 2816), 1), ("w_down", (1408, 2048), 0),
)
CONV = (("w_conv_lru", (4, 512), 1), ("w_fconv", (3, 2816), 1))
FIRST_WEIGHTS = ("w_in", "w_ukv")
EARLY_GRADS = ("w_down", "w_up", "w_co", "w_cq", "w_ck", "w_cv", "w_out", "w_o_mla")
SMALL = (
    ("g_pre_mix", (2048,)), ("g_post_mix", (2048,)), ("g_ckv", (512,)), ("b_conv_lru", (2048,)),
    ("w_rg", (16, 128, 128)), ("b_rg", (16, 128)), ("w_ig", (16, 128, 128)), ("b_ig", (16, 128)),
    ("lru_lambda", (2048,)), ("g_pre_x", (2048,)), ("g_post_x", (2048,)), ("g_mem", (2048,)),
    ("g_pre_ffn", (2048,)), ("g_post_ffn", (2048,)), ("b_fconv", (11264,)),
)
WEIGHTS = ("g_pre_mix", "g_post_mix", "w_in", "g_ckv", "w_ukv", "w_o_mla", "w_conv_lru", "b_conv_lru",
           "w_rg", "b_rg", "w_ig", "b_ig", "lru_lambda", "w_o_lru", "w_out", "g_pre_x", "g_post_x", "g_mem",
           "w_cq", "w_ck", "w_cv", "w_co", "g_pre_ffn", "g_post_ffn", "w_up", "w_fconv", "b_fconv", "w_down")
FLAT_W = 1024
SMALL_XCHG_ROWS = 640
SMALL_ROWS = 560


def _params(sem=None):
    return pltpu.CompilerParams(dimension_semantics=sem, vmem_limit_bytes=V7X_VMEM_LIMIT)


def _tile(dim, pref):
    if dim <= pref:
        return dim
    for t in range(pref - pref % LANES, 0, -LANES):
        if dim % t == 0:
            return t
    raise ValueError((dim, pref))


def _mm(name, a, b, mode, out_dtype=F32, tm=1024, tn=1024, tk=None, owners=1, ride=()):
    if mode == "nn":
        (m, k), (_, n) = a.shape, b.shape
    elif mode == "nt":
        (m, k), (n, _) = a.shape, b.shape
    else:
        (k, m), (_, n) = a.shape, b.shape
    tm, tn = _tile(m, tm), _tile(n // owners, tn)
    if tk is None:
        tk = MM_BLOCK_BYTES // max(tm * a.dtype.itemsize, tn * b.dtype.itemsize)
    tk = _tile(k, tk)
    nk = k // tk
    per_owner = n // owners // tn
    if owners == 1:
        out_spec = pl.BlockSpec((tm, tn), lambda i, j, kk: (i, j))
        out_shape = jax.ShapeDtypeStruct((m, n), out_dtype)
    else:
        out_spec = pl.BlockSpec((None, tm, tn), lambda i, j, kk: (j // per_owner, i, j % per_owner))
        out_shape = jax.ShapeDtypeStruct((owners, m, n // owners), out_dtype)
    if mode == "nn":
        a_spec = pl.BlockSpec((tm, tk), lambda i, j, kk: (i, kk))
        b_spec = pl.BlockSpec((tk, tn), lambda i, j, kk: (kk, j))
        dims = (((1,), (0,)), ((), ()))
    elif mode == "nt":
        a_spec = pl.BlockSpec((tm, tk), lambda i, j, kk: (i, kk))
        b_spec = pl.BlockSpec((tn, tk), lambda i, j, kk: (j, kk))
        dims = (((1,), (1,)), ((), ()))
    else:
        a_spec = pl.BlockSpec((tk, tm), lambda i, j, kk: (kk, i))
        b_spec = pl.BlockSpec((tk, tn), lambda i, j, kk: (kk, j))
        dims = (((0,), (0,)), ((), ()))

    def body(a_ref, b_ref, o_ref, acc_ref):
        kk = pl.program_id(2)

        @pl.when(kk == 0)
        def _():
            acc_ref[...] = jnp.zeros_like(acc_ref)

        acc_ref[...] += lax.dot_general(a_ref[...].astype(BF16), b_ref[...].astype(BF16), dims,
                                        preferred_element_type=F32)

        @pl.when(kk == nk - 1)
        def _():
            o_ref[...] = acc_ref[...].astype(o_ref.dtype)

    grid = (m // tm, n // tn, nk)
    if not ride:
        return pl.pallas_call(
            body, name=name, grid=grid,
            in_specs=[a_spec, b_spec], out_specs=out_spec, out_shape=out_shape,
            scratch_shapes=[pltpu.VMEM((tm, tn), F32)],
            compiler_params=_params(("parallel", "parallel", "arbitrary")),
        )(a, b)
    nr = len(ride)
    return pl.pallas_call(
        _with_rider(body, 2, 1, 1, nr, functools.partial(_exchange_ops, scatter=True), grid), name=name, grid=grid,
        in_specs=[a_spec, b_spec] + [HBM] * nr, out_specs=[out_spec] + [HBM] * nr,
        out_shape=[out_shape] + _exchange_shapes(ride, True),
        scratch_shapes=[pltpu.VMEM((tm, tn), F32)] + _exchange_sems(nr),
        compiler_params=_params(("arbitrary",) * 3),
    )(a, b, *ride)


def _rows(name, fn, n_rows, tm, row_ins, full_ins, row_outs, acc_outs=()):
    in_specs, args = [], []
    for arr, width, cb in row_ins:
        in_specs.append(pl.BlockSpec((tm, width), lambda i, cb=cb: (i, cb)))
        args.append(arr)
    for arr in full_ins:
        in_specs.append(pl.BlockSpec(arr.shape, lambda i, nd=arr.ndim: (0,) * nd))
        args.append(arr)
    out_shape = [jax.ShapeDtypeStruct((n_rows, w), dt) for w, dt in row_outs]
    out_specs = [pl.BlockSpec((tm, w), lambda i: (i, 0)) for w, _ in row_outs]
    for s in acc_outs:
        out_shape.append(jax.ShapeDtypeStruct(s, F32))
        out_specs.append(pl.BlockSpec(s, lambda i, nd=len(s): (0,) * nd))
    n_in, n_ro = len(args), len(row_outs)

    def body(*refs):
        res = fn(*[r[...] for r in refs[:n_in]])
        outs = refs[n_in:]
        for r, v in zip(outs[:n_ro], res[:n_ro]):
            r[...] = v.astype(r.dtype)
        first = pl.program_id(0) == 0
        for r, v in zip(outs[n_ro:], res[n_ro:]):
            @pl.when(first)
            def _(r=r):
                r[...] = jnp.zeros_like(r)

            r[...] += v

    return pl.pallas_call(
        body, name=name, grid=(n_rows // tm,), in_specs=in_specs, out_specs=out_specs, out_shape=out_shape,
        compiler_params=_params(("arbitrary",)),
    )(*args)


def _rsq(x):
    return lax.rsqrt(jnp.mean(x * x, axis=-1, keepdims=True) + EPS)


def _rms(x, g):
    return x * _rsq(x) * g


def _rms_bwd(x, g, dy):
    r = _rsq(x)
    xh = x * r
    dxh = dy * g
    dx = r * (dxh - xh * jnp.mean(dxh * xh, axis=-1, keepdims=True))
    return dx, jnp.sum(dy * xh, axis=0, keepdims=True)


def _sigmoid(x):
    return 1.0 / (1.0 + jnp.exp(-x))


_GELU_C = math.sqrt(2.0 / math.pi)


def _gelu(x):
    return 0.5 * x * (1.0 + jnp.tanh(_GELU_C * (x + 0.044715 * x * x * x)))


def _gelu_and_grad(x):
    th = jnp.tanh(_GELU_C * (x + 0.044715 * x * x * x))
    g = 0.5 * x * (1.0 + th)
    dg = 0.5 * (1.0 + th) + 0.5 * x * (1.0 - th * th) * _GELU_C * (1.0 + 3.0 * 0.044715 * x * x)
    return g, dg


def _dot(a, b):
    return jnp.dot(a.astype(BF16), b.astype(BF16), preferred_element_type=F32)


def _dot_nt(a, b):
    return lax.dot_general(a.astype(BF16), b.astype(BF16), (((1,), (1,)), ((), ())), preferred_element_type=F32)


def _dot_tn(a, b):
    return lax.dot_general(a.astype(BF16), b.astype(BF16), (((0,), (0,)), ((), ())), preferred_element_type=F32)


def _row_iota(shape):
    return lax.broadcasted_iota(jnp.int32, shape, 0)


def _shift_down(x, k, halo):
    xs = pltpu.roll(x, k, 0)
    hs = pltpu.roll(halo, k, 0)
    first = jnp.where(_row_iota(hs.shape) < k, hs, xs[:SUBLANES])
    return jnp.concatenate([first, xs[SUBLANES:]], axis=0)


def _shift_up(x, k, halo):
    tm = x.shape[0]
    xs = pltpu.roll(x, tm - k, 0)
    hs = pltpu.roll(halo, SUBLANES - k, 0)
    last = jnp.where(_row_iota(hs.shape) >= SUBLANES - k, hs, xs[tm - SUBLANES:])
    return jnp.concatenate([xs[:tm - SUBLANES], last], axis=0)


def _conv_taps(x, halo, n_taps):
    return [x] + [_shift_down(x, k, halo) for k in range(1, n_taps)]


def _causal_conv(taps, w, b):
    kw = len(taps)
    y = b + w[kw - 1:kw] * taps[0]
    for s in range(1, kw):
        y = y + w[kw - 1 - s:kw - s] * taps[s]
    return y


def _rope_group(g, c, s1, s2):
    return g * c + pltpu.roll(g, LANES - ROPE_HALF, 1) * s1 + pltpu.roll(g, ROPE_HALF, 1) * s2


def _rope_heads(q, c, s1, s2):
    return jnp.concatenate(
        [_rope_group(q[:, h * HEAD_W:(h + 1) * HEAD_W], c, s1, s2) for h in range(N_HEADS)], axis=1)


def _rope_tables(pos_col, tm):
    inv_freq = ROPE_THETA ** (-jnp.arange(0, 2 * ROPE_HALF, 2, dtype=F32) / (2 * ROPE_HALF))
    invf = jnp.concatenate([inv_freq, inv_freq, jnp.zeros((LANES - 2 * ROPE_HALF,), F32)])[None, :]

    def fn(pos, invf):
        ang = pos.astype(F32) * invf
        c, s = jnp.cos(ang), jnp.sin(ang)
        lane = lax.broadcasted_iota(jnp.int32, ang.shape, 1)
        s1 = jnp.where(lane < ROPE_HALF, -s, 0.0)
        s2 = jnp.where((lane >= ROPE_HALF) & (lane < 2 * ROPE_HALF), s, 0.0)
        return c, s1, s2

    n = pos_col.shape[0]
    return _rows("rope_tables", fn, n, tm, [(pos_col, 1, 0)], [invf], [(LANES, F32)] * 3)


def _attn_tiles(t):
    tq = min(ATTN_TILE, t)
    nq = t // tq
    assert nq == 1 or nq % 2 == 0
    return tq, nq, max(nq // 2, 1), (nq + 1 if nq > 1 else 1)


def _row_chunks(tq):
    c = min(ATTN_CHUNK, tq)
    return [(r0, c) for r0 in range(0, tq, c)]


def _tri_q(p, s, nq):
    first = s <= p
    return jnp.where(first, p, nq - 1 - p), jnp.where(first, s, s - p - 1)


def _tri_k(p, s, nq):
    first = s < nq - p
    return jnp.where(first, p + s, s - 1), jnp.where(first, p, nq - 1 - p)


def _qk(qn_ref, qr_ref, kv_ref, kr_ref):
    q = jnp.concatenate([qn_ref[...].astype(BF16), qr_ref[...]], axis=1)
    k = jnp.concatenate([kv_ref[:, :HEAD_W], kr_ref[...]], axis=1)
    return q, k


def _scores(q, k, r0, diag):
    s = _dot_nt(q, k) * SOFTMAX_SCALE
    if diag:
        row = r0 + lax.broadcasted_iota(jnp.int32, s.shape, 0)
        col = lax.broadcasted_iota(jnp.int32, s.shape, 1)
        s = jnp.where(col <= row, s, -1e30)
    return s


def _with_rider(body, n_in, n_out, n_scratch, n, make_ops, grid):
    def wrapped(*refs):
        a, b, c, d = n_in + n, n_in + n + n_out, n_in + 2 * n + n_out, n_in + 2 * n + n_out + n_scratch
        start, finish = make_ops(refs[n_in:a], refs[b:c], *refs[d:])
        ids = [pl.program_id(ax) for ax in range(len(grid))]
        first = functools.reduce(jnp.logical_and, [i == 0 for i in ids])
        last = functools.reduce(jnp.logical_and, [i == g - 1 for i, g in zip(ids, grid)])

        @pl.when(first)
        def _():
            start()

        body(*refs[:n_in], *refs[a:b], *refs[c:d])

        @pl.when(last)
        def _():
            finish()

    return wrapped


def _flash_fwd(proj, qr, kv, kr, ride=()):
    t = proj.shape[0]
    tq, nq, n_pairs, n_steps = _attn_tiles(t)
    n = len(ride)
    grid = (N_HEADS, n_pairs, n_steps)

    def body(qn_ref, qr_ref, kv_ref, kr_ref, o_ref, lse_ref, m_s, l_s, acc_s):
        qi, ki = _tri_q(pl.program_id(1), pl.program_id(2), nq)

        @pl.when(ki == 0)
        def _():
            m_s[...] = jnp.full_like(m_s, -1e30)
            l_s[...] = jnp.zeros_like(l_s)
            acc_s[...] = jnp.zeros_like(acc_s)

        def step(diag):
            q, k = _qk(qn_ref, qr_ref, kv_ref, kr_ref)
            for r0, c in _row_chunks(tq):
                rows = slice(r0, r0 + c)
                n = r0 + c if diag else tq
                s = _scores(q[rows], k[:n], r0, diag)
                m_prev = m_s[rows]
                m_new = jnp.maximum(m_prev, jnp.max(s, axis=1, keepdims=True))
                alpha = jnp.exp(m_prev - m_new)
                p = jnp.exp(s - m_new)
                l_s[rows] = alpha * l_s[rows] + jnp.sum(p, axis=1, keepdims=True)
                acc_s[rows] = alpha * acc_s[rows] + _dot(p, kv_ref[:n, HEAD_W:])
                m_s[rows] = m_new

        @pl.when(ki < qi)
        def _():
            step(False)

        @pl.when(ki == qi)
        def _():
            step(True)
            o_ref[...] = acc_s[...] / l_s[...]
            lse_ref[...] = m_s[...] + jnp.log(l_s[...])

    qb = lambda p, s: _tri_q(p, s, nq)[0]
    kb = lambda p, s: _tri_q(p, s, nq)[1]
    return pl.pallas_call(
        _with_rider(body, 4, 2, 3, n, _gather_chips_ops, grid) if n else body, name="mla_fwd", grid=grid,
        in_specs=[pl.BlockSpec((tq, HEAD_W), lambda h, p, s: (qb(p, s), h)),
                  pl.BlockSpec((tq, HEAD_W), lambda h, p, s: (qb(p, s), h)),
                  pl.BlockSpec((tq, 2 * HEAD_W), lambda h, p, s: (kb(p, s), h)),
                  pl.BlockSpec((tq, HEAD_W), lambda h, p, s: (kb(p, s), 0))] + [HBM] * n,
        out_specs=[pl.BlockSpec((tq, HEAD_W), lambda h, p, s: (qb(p, s), h)),
                   pl.BlockSpec((None, tq, 1), lambda h, p, s: (h, qb(p, s), 0))] + [HBM] * n,
        out_shape=[jax.ShapeDtypeStruct((t, N_HEADS * HEAD_W), F32),
                   jax.ShapeDtypeStruct((N_HEADS, t, 1), F32)]
        + [jax.ShapeDtypeStruct((4,) + a.shape, a.dtype) for a in ride],
        scratch_shapes=[pltpu.VMEM((tq, 1), F32), pltpu.VMEM((tq, 1), F32), pltpu.VMEM((tq, HEAD_W), F32)]
        + (_gather_sems(n) if n else []),
        compiler_params=_params(("arbitrary",) * 3 if n else ("parallel", "parallel", "arbitrary")),
    )(proj, qr, kv, kr, *ride)


def _flash_dq(proj, qr, kv, kr, do, o, lse, ride=()):
    t = proj.shape[0]
    tq, nq, n_pairs, n_steps = _attn_tiles(t)
    n = len(ride)
    grid = (N_HEADS, n_pairs, n_steps)
    scatter_ops = functools.partial(_exchange_ops, scatter=True)

    def body(qn_ref, qr_ref, kv_ref, kr_ref, do_ref, o_ref, lse_ref, dqn_ref, dqr_ref, dq_s, dsum_s):
        qi, ki = _tri_q(pl.program_id(1), pl.program_id(2), nq)

        @pl.when(ki == 0)
        def _():
            dq_s[...] = jnp.zeros_like(dq_s)
            dsum_s[...] = jnp.sum(do_ref[...] * o_ref[...], axis=1, keepdims=True)

        def step(diag):
            q, k = _qk(qn_ref, qr_ref, kv_ref, kr_ref)
            for r0, c in _row_chunks(tq):
                rows = slice(r0, r0 + c)
                n = r0 + c if diag else tq
                p = jnp.exp(_scores(q[rows], k[:n], r0, diag) - lse_ref[rows])
                dp = _dot_nt(do_ref[rows], kv_ref[:n, HEAD_W:])
                ds = p * (dp - dsum_s[rows]) * SOFTMAX_SCALE
                dq_s[rows] += _dot(ds, k[:n])

        @pl.when(ki < qi)
        def _():
            step(False)

        @pl.when(ki == qi)
        def _():
            step(True)
            dqn_ref[...] = dq_s[:, :HEAD_W].astype(dqn_ref.dtype)
            dqr_ref[...] = dq_s[:, HEAD_W:]

    qb = lambda p, s: _tri_q(p, s, nq)[0]
    kb = lambda p, s: _tri_q(p, s, nq)[1]
    qspec = pl.BlockSpec((tq, HEAD_W), lambda h, p, s: (qb(p, s), h))
    return pl.pallas_call(
        _with_rider(body, 7, 2, 2, n, scatter_ops, grid) if n else body, name="mla_dq", grid=grid,
        in_specs=[qspec, qspec,
                  pl.BlockSpec((tq, 2 * HEAD_W), lambda h, p, s: (kb(p, s), h)),
                  pl.BlockSpec((tq, HEAD_W), lambda h, p, s: (kb(p, s), 0)),
                  qspec, qspec,
                  pl.BlockSpec((None, tq, 1), lambda h, p, s: (h, qb(p, s), 0))] + [HBM] * n,
        out_specs=[qspec, qspec] + [HBM] * n,
        out_shape=[jax.ShapeDtypeStruct((t, N_HEADS * HEAD_W), BF16),
                   jax.ShapeDtypeStruct((t, N_HEADS * HEAD_W), F32)] + _exchange_shapes(ride, True),
        scratch_shapes=[pltpu.VMEM((tq, 2 * HEAD_W), F32), pltpu.VMEM((tq, 1), F32)]
        + (_exchange_sems(n) if n else []),
        compiler_params=_params(("arbitrary",) * 3 if n else ("parallel", "parallel", "arbitrary")),
    )(proj, qr, kv, kr, do, o, lse, *ride)


def _flash_dkv(proj, qr, kv, kr, do, o, lse, ride=()):
    t = proj.shape[0]
    tq, nq, n_pairs, n_steps = _attn_tiles(t)
    n = len(ride)
    grid = (N_HEADS, n_pairs, n_steps)
    chunks = _core_chunks(ride)
    swap_ops = functools.partial(_gather_cores_ops, chunks=chunks)

    def body(qn_ref, qr_ref, kv_ref, kr_ref, do_ref, o_ref, lse_ref, dkv_ref, dkr_ref, dk_s, dv_s):
        qi, ki = _tri_k(pl.program_id(1), pl.program_id(2), nq)

        @pl.when(qi == ki)
        def _():
            dk_s[...] = jnp.zeros_like(dk_s)
            dv_s[...] = jnp.zeros_like(dv_s)

        def step(diag):
            q, k = _qk(qn_ref, qr_ref, kv_ref, kr_ref)
            for r0, c in _row_chunks(tq):
                rows = slice(r0, r0 + c)
                n = r0 + c if diag else tq
                p = jnp.exp(_scores(q[rows], k[:n], r0, diag) - lse_ref[rows])
                do = do_ref[rows]
                dsum = jnp.sum(do * o_ref[rows], axis=1, keepdims=True)
                dv_s[:n] += _dot_tn(p, do)
                dp = _dot_nt(do, kv_ref[:n, HEAD_W:])
                ds = p * (dp - dsum) * SOFTMAX_SCALE
                dk_s[:n] += _dot_tn(ds, q[rows])

        @pl.when(qi == ki)
        def _():
            step(True)

        @pl.when(qi > ki)
        def _():
            step(False)

        @pl.when(qi == nq - 1)
        def _():
            dkv_ref[...] = jnp.concatenate([dk_s[:, :HEAD_W], dv_s[...]], axis=1).astype(dkv_ref.dtype)
            dkr_ref[...] = dk_s[:, HEAD_W:]

    qb = lambda p, s: _tri_k(p, s, nq)[0]
    kb = lambda p, s: _tri_k(p, s, nq)[1]
    qspec = pl.BlockSpec((tq, HEAD_W), lambda h, p, s: (qb(p, s), h))
    return pl.pallas_call(
        _with_rider(body, 7, 2, 2, n, swap_ops, grid) if n else body, name="mla_dkv", grid=grid,
        in_specs=[qspec, qspec,
                  pl.BlockSpec((tq, 2 * HEAD_W), lambda h, p, s: (kb(p, s), h)),
                  pl.BlockSpec((tq, HEAD_W), lambda h, p, s: (kb(p, s), 0)),
                  qspec, qspec,
                  pl.BlockSpec((None, tq, 1), lambda h, p, s: (h, qb(p, s), 0))] + [HBM] * n,
        out_specs=[pl.BlockSpec((tq, 2 * HEAD_W), lambda h, p, s: (kb(p, s), h)),
                   pl.BlockSpec((None, tq, HEAD_W), lambda h, p, s: (h, kb(p, s), 0))] + [HBM] * n,
        out_shape=[jax.ShapeDtypeStruct((t, N_HEADS * 2 * HEAD_W), BF16),
                   jax.ShapeDtypeStruct((N_HEADS, t, HEAD_W), F32)] + _core_shapes(ride),
        scratch_shapes=[pltpu.VMEM((tq, 2 * HEAD_W), F32), pltpu.VMEM((tq, HEAD_W), F32)]
        + (_core_sems(chunks) if n else []),
        compiler_params=_params(("arbitrary",) * 3 if n else ("parallel", "parallel", "arbitrary")),
    )(proj, qr, kv, kr, do, o, lse, *ride)


def _block_diag(x, w):
    return jnp.concatenate(
        [_dot(x[:, b * LANES:(b + 1) * LANES], w[b]) for b in range(LRU_BLOCKS)], axis=1)


def _block_diag_t(d, w):
    return jnp.concatenate(
        [_dot_nt(d[:, b * LANES:(b + 1) * LANES], w[b]) for b in range(LRU_BLOCKS)], axis=1)


def _lru_gates(xc, w_rg, b_rg, w_ig, b_ig, lam):
    r = _sigmoid(_block_diag(xc, w_rg) + b_rg)
    ig = _sigmoid(_block_diag(xc, w_ig) + b_ig)
    sp = jnp.maximum(-lam, 0.0) + jnp.log1p(jnp.exp(-jnp.abs(lam)))
    log_a = -LRU_C * r * sp
    a = jnp.exp(log_a)
    mult = jnp.sqrt(-jnp.tanh(log_a) * (a * a + 1.0))
    return r, ig, sp, a, mult


def _lru_tm(t):
    return min(128, t)


def _halo_spec(tm, width, cb):
    return pl.BlockSpec((SUBLANES, width), lambda i, cb=cb: (jnp.maximum(i * (tm // SUBLANES) - 1, 0), cb))


def _lru_fwd(proj, w_conv, b_conv, w_rg, b_rg, w_ig, b_ig, lam):
    t = proj.shape[0]
    tm = _lru_tm(t)
    w = D_MODEL
    n_scan = int(math.log2(tm))

    def body(lx_ref, halo_ref, ly_ref, wc_ref, bc_ref, wrg_ref, brg_ref, wig_ref, big_ref, lam_ref,
             oli_ref, h_ref, carry_s):
        i = pl.program_id(0)

        @pl.when(i == 0)
        def _():
            carry_s[...] = jnp.zeros_like(carry_s)

        x = lx_ref[...]
        halo = jnp.where(i > 0, halo_ref[...], 0.0)
        xc = _causal_conv(_conv_taps(x, halo, 4), wc_ref[...], bc_ref[...])
        _, ig, _, a, mult = _lru_gates(xc, wrg_ref[...], brg_ref[...], wig_ref[...], big_ref[...], lam_ref[...])
        u = mult * (ig * xc)
        rows = _row_iota(a.shape)
        for s in range(n_scan):
            d = 1 << s
            keep = rows >= d
            a_s = jnp.where(keep, pltpu.roll(a, d, 0), 1.0)
            u_s = jnp.where(keep, pltpu.roll(u, d, 0), 0.0)
            u = a * u_s + u
            a = a * a_s
        h = u + a * carry_s[SUBLANES - 1:SUBLANES, :]
        h_ref[...] = h
        carry_s[...] = h[tm - SUBLANES:, :]
        oli_ref[...] = (h * _gelu(ly_ref[...])).astype(oli_ref.dtype)

    full = lambda arr: pl.BlockSpec(arr.shape, lambda i, nd=arr.ndim: (0,) * nd)
    return pl.pallas_call(
        body, name="lru_fwd", grid=(t // tm,),
        in_specs=[pl.BlockSpec((tm, w), lambda i: (i, SEG_LX)), _halo_spec(tm, w, SEG_LX),
                  pl.BlockSpec((tm, w), lambda i: (i, SEG_LY)),
                  full(w_conv), full(b_conv), full(w_rg), full(b_rg), full(w_ig), full(b_ig), full(lam)],
        out_specs=[pl.BlockSpec((tm, w), lambda i: (i, 0)), pl.BlockSpec((tm, w), lambda i: (i, 0))],
        out_shape=[jax.ShapeDtypeStruct((t, w), BF16), jax.ShapeDtypeStruct((t, w), F32)],
        scratch_shapes=[pltpu.VMEM((SUBLANES, w), F32)],
        compiler_params=_params(("arbitrary",)),
    )(proj, proj, proj, w_conv, b_conv, w_rg, b_rg, w_ig, b_ig, lam)


def _lru_bwd(proj, hl, doli, w_conv, b_conv, w_rg, b_rg, w_ig, b_ig, lam):
    t = proj.shape[0]
    tm = _lru_tm(t)
    nt = t // tm
    w = D_MODEL
    n_scan = int(math.log2(tm))

    def body(lx_ref, lxh_ref, ly_ref, h_ref, hh_ref, doli_ref,
             wc_ref, bc_ref, wrg_ref, brg_ref, wig_ref, big_ref, lam_ref,
             dlx_ref, dly_ref, dwc_ref, dbc_ref, dwrg_ref, dbrg_ref, dwig_ref, dbig_ref, dlam_ref,
             ca_s, cd_s, cx_s):
        i = pl.program_id(0)
        blk = nt - 1 - i

        @pl.when(i == 0)
        def _():
            for r in (ca_s, cd_s, cx_s, dwc_ref, dbc_ref, dwrg_ref, dbrg_ref, dwig_ref, dbig_ref, dlam_ref):
                r[...] = jnp.zeros_like(r)

        x = lx_ref[...]
        halo = jnp.where(blk > 0, lxh_ref[...], 0.0)
        taps = _conv_taps(x, halo, 4)
        wc = wc_ref[...]
        xc = _causal_conv(taps, wc, bc_ref[...])
        w_rg, w_ig, lam_v = wrg_ref[...], wig_ref[...], lam_ref[...]
        r, ig, sp, a, mult = _lru_gates(xc, w_rg, brg_ref[...], w_ig, big_ref[...], lam_v)
        h = h_ref[...]
        h_prev = _shift_down(h, 1, jnp.where(blk > 0, hh_ref[...], 0.0))
        gl, dgl = _gelu_and_grad(ly_ref[...])
        doli = doli_ref[...]
        dly_ref[...] = (doli * h * dgl).astype(dly_ref.dtype)
        acc_b = doli * gl
        acc_a = _shift_up(a, 1, ca_s[...])
        rows = _row_iota(a.shape)
        for s in range(n_scan):
            d = 1 << s
            keep = rows < tm - d
            a_s = jnp.where(keep, pltpu.roll(acc_a, tm - d, 0), 1.0)
            b_s = jnp.where(keep, pltpu.roll(acc_b, tm - d, 0), 0.0)
            acc_b = acc_b + acc_a * b_s
            acc_a = acc_a * a_s
        dht = acc_b + acc_a * cd_s[0:1, :]
        da = dht * h_prev
        dmult = dht * (ig * xc)
        di = dht * mult * xc
        dxc = dht * mult * ig
        dlog_a = da * a - dmult * (a * a) / mult
        dr = dlog_a * (-LRU_C * sp)
        dlam_ref[...] += jnp.sum(dlog_a * r, axis=0, keepdims=True) * (LRU_C * _sigmoid(-lam_v))
        dpr = dr * r * (1.0 - r)
        dpi = di * ig * (1.0 - ig)
        dxc = dxc + _block_diag_t(dpr, w_rg) + _block_diag_t(dpi, w_ig)
        dbrg_ref[...] += jnp.sum(dpr, axis=0, keepdims=True)
        dbig_ref[...] += jnp.sum(dpi, axis=0, keepdims=True)
        for b in range(LRU_BLOCKS):
            sl = slice(b * LANES, (b + 1) * LANES)
            dwrg_ref[b] += _dot_tn(xc[:, sl], dpr[:, sl])
            dwig_ref[b] += _dot_tn(xc[:, sl], dpi[:, sl])
        dbc_ref[...] += jnp.sum(dxc, axis=0, keepdims=True)
        dwc_ref[...] += jnp.concatenate(
            [jnp.sum(dxc * taps[3 - k], axis=0, keepdims=True) for k in range(4)], axis=0)
        cx = cx_s[...]
        dlx = wc[3:4] * dxc
        for s in range(1, 4):
            dlx = dlx + wc[3 - s:4 - s] * _shift_up(dxc, s, cx)
        dlx_ref[...] = dlx.astype(dlx_ref.dtype)
        ca_s[...] = a[:SUBLANES, :]
        cd_s[...] = dht[:SUBLANES, :]
        cx_s[...] = dxc[:SUBLANES, :]

    full = lambda arr: pl.BlockSpec(arr.shape, lambda i, nd=arr.ndim: (0,) * nd)
    rev = lambda cb: pl.BlockSpec((tm, w), lambda i, cb=cb: (nt - 1 - i, cb))
    halo = lambda cb: pl.BlockSpec(
        (SUBLANES, w), lambda i, cb=cb: (jnp.maximum((nt - 1 - i) * (tm // SUBLANES) - 1, 0), cb))
    acc = lambda s: pl.BlockSpec(s, lambda i, nd=len(s): (0,) * nd)
    acc_shapes = [(4, w), (1, w), (LRU_BLOCKS, LANES, LANES), (1, w), (LRU_BLOCKS, LANES, LANES), (1, w), (1, w)]
    return pl.pallas_call(
        body, name="lru_bwd", grid=(nt,),
        in_specs=[rev(SEG_LX), halo(SEG_LX), rev(SEG_LY), rev(0), halo(0), rev(0),
                  full(w_conv), full(b_conv), full(w_rg), full(b_rg), full(w_ig), full(b_ig), full(lam)],
        out_specs=[rev(0), rev(0)] + [acc(s) for s in acc_shapes],
        out_shape=[jax.ShapeDtypeStruct((t, w), BF16), jax.ShapeDtypeStruct((t, w), BF16)]
        + [jax.ShapeDtypeStruct(s, F32) for s in acc_shapes],
        scratch_shapes=[pltpu.VMEM((SUBLANES, w), F32)] * 3,
        compiler_params=_params(("arbitrary",)),
    )(proj, proj, proj, hl, hl, doli, w_conv, b_conv, w_rg, b_rg, w_ig, b_ig, lam)


FFN_TC = 512
FFN_NJ = D_FF // FFN_TC


def _ffn_tm(t):
    return min(512, t)


def _ffn_fwd(up_pre, w_fconv, b_fconv):
    t = up_pre.shape[0]
    tm = _ffn_tm(t)

    def body(g_ref, gh_ref, v_ref, vh_ref, wg_ref, wv_ref, bg_ref, bv_ref, act_ref):
        first = pl.program_id(0) > 0
        gate = _causal_conv(_conv_taps(g_ref[...], jnp.where(first, gh_ref[...], 0.0), 3), wg_ref[...], bg_ref[...])
        val = _causal_conv(_conv_taps(v_ref[...], jnp.where(first, vh_ref[...], 0.0), 3), wv_ref[...], bv_ref[...])
        act_ref[...] = (_gelu(gate) * val).astype(act_ref.dtype)

    blk = lambda off: pl.BlockSpec((tm, FFN_TC), lambda i, j: (i, j + off))
    halo = lambda off: pl.BlockSpec(
        (SUBLANES, FFN_TC), lambda i, j: (jnp.maximum(i * (tm // SUBLANES) - 1, 0), j + off))
    wsp = lambda rows, off: pl.BlockSpec((rows, FFN_TC), lambda i, j: (0, j + off))
    return pl.pallas_call(
        body, name="ffn_conv_fwd", grid=(t // tm, FFN_NJ),
        in_specs=[blk(0), halo(0), blk(FFN_NJ), halo(FFN_NJ), wsp(3, 0), wsp(3, FFN_NJ), wsp(1, 0), wsp(1, FFN_NJ)],
        out_specs=pl.BlockSpec((tm, FFN_TC), lambda i, j: (i, j)),
        out_shape=jax.ShapeDtypeStruct((t, D_FF), BF16),
        compiler_params=_params(("parallel", "parallel")),
    )(up_pre, up_pre, up_pre, up_pre, w_fconv, w_fconv, b_fconv, b_fconv)


def _ffn_bwd_act(up_pre, dact, w_fconv, b_fconv):
    t = up_pre.shape[0]
    tm = _ffn_tm(t)

    def body(g_ref, gh_ref, v_ref, vh_ref, da_ref, wg_ref, wv_ref, bg_ref, bv_ref, dup_ref, dw_ref, db_ref):
        jj, i = pl.program_id(0), pl.program_id(1)

        @pl.when(i == 0)
        def _():
            dw_ref[...] = jnp.zeros_like(dw_ref)
            db_ref[...] = jnp.zeros_like(db_ref)

        g_taps = _conv_taps(g_ref[...], jnp.where(i > 0, gh_ref[...], 0.0), 3)
        v_taps = _conv_taps(v_ref[...], jnp.where(i > 0, vh_ref[...], 0.0), 3)
        gate = _causal_conv(g_taps, wg_ref[...], bg_ref[...])
        val = _causal_conv(v_taps, wv_ref[...], bv_ref[...])
        gl, dgl = _gelu_and_grad(gate)
        dact = da_ref[...]
        is_gate = jj < FFN_NJ
        d = jnp.where(is_gate, dact * val * dgl, dact * gl)
        dup_ref[...] = d
        db_ref[...] += jnp.sum(d, axis=0, keepdims=True)
        dw_ref[...] += jnp.concatenate(
            [jnp.sum(d * jnp.where(is_gate, g_taps[2 - k], v_taps[2 - k]), axis=0, keepdims=True)
             for k in range(3)], axis=0)

    jm = lambda jj: jj % FFN_NJ
    blk = lambda off: pl.BlockSpec((tm, FFN_TC), lambda jj, i: (i, jm(jj) + off))
    halo = lambda off: pl.BlockSpec(
        (SUBLANES, FFN_TC), lambda jj, i: (jnp.maximum(i * (tm // SUBLANES) - 1, 0), jm(jj) + off))
    wsp = lambda rows, off: pl.BlockSpec((rows, FFN_TC), lambda jj, i: (0, jm(jj) + off))
    return pl.pallas_call(
        body, name="ffn_bwd_act", grid=(2 * FFN_NJ, t // tm),
        in_specs=[blk(0), halo(0), blk(FFN_NJ), halo(FFN_NJ), blk(0),
                  wsp(3, 0), wsp(3, FFN_NJ), wsp(1, 0), wsp(1, FFN_NJ)],
        out_specs=[pl.BlockSpec((tm, FFN_TC), lambda jj, i: (i, jj)),
                   pl.BlockSpec((3, FFN_TC), lambda jj, i: (0, jj)),
                   pl.BlockSpec((1, FFN_TC), lambda jj, i: (0, jj))],
        out_shape=[jax.ShapeDtypeStruct((t, 2 * D_FF), F32), jax.ShapeDtypeStruct((3, 2 * D_FF), F32),
                   jax.ShapeDtypeStruct((1, 2 * D_FF), F32)],
        compiler_params=_params(("parallel", "arbitrary")),
    )(up_pre, up_pre, up_pre, up_pre, dact, w_fconv, w_fconv, b_fconv, b_fconv)


def _ffn_bwd_conv(dup, w_fconv):
    t = dup.shape[0]
    tm = _ffn_tm(t)
    nt = t // tm

    def body(d_ref, dh_ref, w_ref, out_ref):
        d = d_ref[...]
        halo = jnp.where(pl.program_id(0) < nt - 1, dh_ref[...], 0.0)
        wv = w_ref[...]
        out = wv[2:3] * d + wv[1:2] * _shift_up(d, 1, halo) + wv[0:1] * _shift_up(d, 2, halo)
        out_ref[...] = out.astype(out_ref.dtype)

    return pl.pallas_call(
        body, name="ffn_bwd_conv", grid=(nt, 2 * FFN_NJ),
        in_specs=[pl.BlockSpec((tm, FFN_TC), lambda i, j: (i, j)),
                  pl.BlockSpec((SUBLANES, FFN_TC),
                               lambda i, j: (jnp.minimum((i + 1) * (tm // SUBLANES), t // SUBLANES - 1), j)),
                  pl.BlockSpec((3, FFN_TC), lambda i, j: (0, j))],
        out_specs=pl.BlockSpec((tm, FFN_TC), lambda i, j: (i, j)),
        out_shape=jax.ShapeDtypeStruct((t, 2 * D_FF), BF16),
        compiler_params=_params(("parallel", "parallel")),
    )(dup, dup, w_fconv)


def _xattn_probs(cq, ck, h):
    sl = slice(h * LANES, (h + 1) * LANES)
    s = _dot_nt(cq[:, sl], ck[:, sl]) * X_SCALE
    e = jnp.exp(s - jnp.max(s, axis=1, keepdims=True))
    return e / jnp.sum(e, axis=1, keepdims=True), sl


def _xattn_fwd_fn(cq, ck, cv):
    outs = []
    for h in range(X_HEADS):
        p, sl = _xattn_probs(cq, ck, h)
        outs.append(_dot(p, cv[:, sl]))
    return (jnp.concatenate(outs, axis=1),)


def _xattn_bwd_fn(cq, dco, ck, cv):
    dcq, dck, dcv = [], [], []
    for h in range(X_HEADS):
        p, sl = _xattn_probs(cq, ck, h)
        dcv.append(_dot_tn(p, dco[:, sl]))
        dp = _dot_nt(dco[:, sl], cv[:, sl])
        ds = p * (dp - jnp.sum(p * dp, axis=1, keepdims=True)) * X_SCALE
        dcq.append(_dot(ds, ck[:, sl]))
        dck.append(_dot_tn(ds, cq[:, sl]))
    return jnp.concatenate(dcq, axis=1), jnp.concatenate(dck, axis=1), jnp.concatenate(dcv, axis=1)


def _perm_w_in(w):
    q = w[:, :3072].reshape(D_MODEL, N_HEADS, 192)
    qn = q[:, :, :128].reshape(D_MODEL, 2048)
    qr = jnp.pad(q[:, :, 128:], ((0, 0), (0, 0), (0, 64))).reshape(D_MODEL, 2048)
    ckv = w[:, 3072:3584]
    kr = jnp.pad(w[:, 3584:3648], ((0, 0), (0, 64)))
    rest = w[:, 3648:]
    pad = jnp.zeros((D_MODEL, PROJ_W - 12928), w.dtype)
    return jnp.concatenate([qn, qr, rest, ckv, kr, pad], axis=1)


def _unperm_dw_in(dw):
    qn = dw[:, :2048].reshape(D_MODEL, N_HEADS, 128)
    qr = dw[:, 2048:4096].reshape(D_MODEL, N_HEADS, 128)[:, :, :64]
    q = jnp.concatenate([qn, qr], axis=2).reshape(D_MODEL, 3072)
    return jnp.concatenate([q, dw[:, 12288:12800], dw[:, 12800:12864], dw[:, 4096:12288]], axis=1)


def _whole(defs, gathered):
    return {n: g.reshape(-1, g.shape[-1]) if ax == 0 else jnp.concatenate([g[j] for j in range(4)], axis=1)
            for (n, _, ax), g in zip(defs, gathered)}


def _local_step(x, mem, positions, tgt, wb, ws, late_weights=(), early_grads=(), late_grads=()):
    t = x.shape[0]
    n_mem = mem.shape[0]
    tm = min(256, t)
    row = lambda v: v.reshape(1, -1)
    g_pre_mix, g_post_mix, g_ckv = row(ws["g_pre_mix"]), row(ws["g_post_mix"]), row(ws["g_ckv"])
    g_pre_x, g_post_x, g_mem = row(ws["g_pre_x"]), row(ws["g_post_x"]), row(ws["g_mem"])
    g_pre_ffn, g_post_ffn = row(ws["g_pre_ffn"]), row(ws["g_post_ffn"])
    b_conv, lam, b_fconv = row(ws["b_conv_lru"]), row(ws["lru_lambda"]), row(ws["b_fconv"])
    b_rg, b_ig = row(ws["b_rg"]), row(ws["b_ig"])
    w_rg, w_ig = ws["w_rg"].astype(BF16), ws["w_ig"].astype(BF16)
    w_in = _perm_w_in(wb["w_in"])
    W = D_MODEL

    def dw(name, a, b, owners=1, tn=1024):
        g = _mm(name, a, b, "tn", out_dtype=BF16, owners=owners, tn=tn)
        return g.reshape(8, -1, g.shape[-1])

    cos_t, s1_t, s2_t = _rope_tables(positions.reshape(t, 1), tm)
    (h1,) = _rows("rms_pre_mix", lambda x, g: (_rms(x, g),), t, tm, [(x, W, 0)], [g_pre_mix], [(W, BF16)])
    proj = _mm("proj", h1, w_in, "nn")

    def rope_ckv_fn(qr, ckv, kr, c, s1, s2, g):
        return _rope_heads(qr, c, s1, s2), _rope_group(kr, c, s1, s2), _rms(ckv, g)

    qr, kr, ckvn = _rows("rope_ckv", rope_ckv_fn, t, tm,
                         [(proj, W, SEG_QR), (proj, 512, CKV_BLOCK), (proj, LANES, KR_BLOCK),
                          (cos_t, LANES, 0), (s1_t, LANES, 0), (s2_t, LANES, 0)], [g_ckv],
                         [(W, BF16), (LANES, BF16), (512, BF16)])
    kv = _mm("kv_up", ckvn, wb["w_ukv"], "nn", out_dtype=BF16)
    o, lse, *gathered = _flash_fwd(proj, qr, kv, kr, ride=[blk for _, blk in late_weights])
    wb = {**wb, **_whole([d for d, _ in late_weights], gathered)}
    w_conv, w_fconv = wb["w_conv_lru"], wb["w_fconv"]
    o_mla = _mm("o_mla", o, wb["w_o_mla"], "nn")
    oli, hl = _lru_fwd(proj, w_conv, b_conv, w_rg, b_rg, w_ig, b_ig, lam)
    o_lru = _mm("o_lru", oli, wb["w_o_lru"], "nn")
    (merged,) = _rows("merge", lambda gm, gl, a, b: (_sigmoid(gm) * a + _sigmoid(gl) * b,), t, tm,
                      [(proj, W, SEG_GM), (proj, W, SEG_GL), (o_mla, W, 0), (o_lru, W, 0)], [], [(W, BF16)])
    z1 = _mm("w_out", merged, wb["w_out"], "nn")

    def post_fn(x, z, g_post, g_pre):
        x1 = x + _rms(z, g_post)
        return x1, _rms(x1, g_pre)

    x1, h2 = _rows("post_mix", post_fn, t, tm, [(x, W, 0), (z1, W, 0)], [g_post_mix, g_pre_x],
                   [(W, F32), (W, BF16)])
    cq = _mm("cq", h2, wb["w_cq"], "nn", out_dtype=BF16)
    (mn,) = _rows("rms_mem", lambda m, g: (_rms(m, g),), n_mem, n_mem, [(mem, W, 0)], [g_mem], [(W, BF16)])
    ck = _mm("ck", mn, wb["w_ck"], "nn", out_dtype=BF16)
    cv = _mm("cv", mn, wb["w_cv"], "nn", out_dtype=BF16)
    (co,) = _rows("xattn_fwd", _xattn_fwd_fn, t, tm, [(cq, 512, 0)], [ck, cv], [(512, BF16)])
    z2 = _mm("w_co", co, wb["w_co"], "nn")
    x2, h3 = _rows("post_x", post_fn, t, tm, [(x1, W, 0), (z2, W, 0)], [g_post_x, g_pre_ffn],
                   [(W, F32), (W, BF16)])
    up_pre = _mm("w_up", h3, wb["w_up"], "nn")
    act = _ffn_fwd(up_pre, w_fconv, b_fconv)
    z3 = _mm("w_down", act, wb["w_down"], "nn")

    def loss_fn(x2, z3, tgt, g):
        err = x2 + _rms(z3, g) - tgt
        dy = err * (1.0 / W)
        dz, dg = _rms_bwd(z3, g, dy)
        part = 0.5 * jnp.sum(err * err) * (1.0 / W)
        return dy, dz, jnp.zeros((SUBLANES, LANES), F32) + part, dg

    dy, dz3, loss_acc, dg_post_ffn = _rows(
        "loss", loss_fn, t, tm, [(x2, W, 0), (z3, W, 0), (tgt, W, 0)], [g_post_ffn],
        [(W, F32), (W, BF16)], [(SUBLANES, LANES), (1, W)])
    grads = {"g_post_ffn": dg_post_ffn}
    dact = _mm("d_act", dz3, wb["w_down"], "nt")
    grads["w_down"] = dw("dw_down", act, dz3)
    dup, grads["w_fconv"], grads["b_fconv"] = _ffn_bwd_act(up_pre, dact, w_fconv, b_fconv)
    dup_pre = _ffn_bwd_conv(dup, w_fconv)
    dh3 = _mm("d_h3", dup_pre, wb["w_up"], "nt")
    grads["w_up"] = dw("dw_up", h3, dup_pre, owners=4, tn=1408)

    def res_bwd_fn(dres, xa, dh, z, g_pre, g_post):
        dxa, dg_pre = _rms_bwd(xa, g_pre, dh)
        dxa = dres + dxa
        dz, dg_post = _rms_bwd(z, g_post, dxa)
        return dxa, dz, dg_pre, dg_post

    dx2, dz2, grads["g_pre_ffn"], grads["g_post_x"] = _rows(
        "bwd_post_x", res_bwd_fn, t, tm, [(dy, W, 0), (x2, W, 0), (dh3, W, 0), (z2, W, 0)],
        [g_pre_ffn, g_post_x], [(W, F32), (W, BF16)], [(1, W), (1, W)])
    dco = _mm("d_co", dz2, wb["w_co"], "nt")
    grads["w_co"] = dw("dw_co", co, dz2, owners=4)
    dcq, dck, dcv = _rows("xattn_bwd", _xattn_bwd_fn, t, tm, [(cq, 512, 0), (dco, 512, 0)], [ck, cv],
                          [(512, BF16)], [(n_mem, 512), (n_mem, 512)])
    dh2 = _mm("d_h2", dcq, wb["w_cq"], "nt")
    grads["w_cq"] = dw("dw_cq", h2, dcq)
    grads["w_ck"] = dw("dw_ck", mn, dck)
    grads["w_cv"] = dw("dw_cv", mn, dcv)
    dmn_k = _mm("d_mn_k", dck, wb["w_ck"], "nt")
    dmn_v = _mm("d_mn_v", dcv, wb["w_cv"], "nt")
    (grads["g_mem"],) = _rows("dg_mem", lambda m, a, b: (jnp.sum((a + b) * m * _rsq(m), axis=0, keepdims=True),),
                              n_mem, n_mem, [(mem, W, 0), (dmn_k, W, 0), (dmn_v, W, 0)], [], [], [(1, W)])
    dx1, dz1, grads["g_pre_x"], grads["g_post_mix"] = _rows(
        "bwd_post_mix", res_bwd_fn, t, tm, [(dx2, W, 0), (x1, W, 0), (dh2, W, 0), (z1, W, 0)],
        [g_pre_x, g_post_mix], [(W, F32), (W, BF16)], [(1, W), (1, W)])
    dmerged = _mm("d_merged", dz1, wb["w_out"], "nt")
    grads["w_out"] = dw("dw_out", merged, dz1)

    def merge_bwd_fn(dm, gm, gl, a, b):
        sm, sl = _sigmoid(gm), _sigmoid(gl)
        return dm * sm, dm * sl, dm * a * sm * (1.0 - sm), dm * b * sl * (1.0 - sl)

    do_mla, do_lru, dgm, dgl = _rows(
        "merge_bwd", merge_bwd_fn, t, tm,
        [(dmerged, W, 0), (proj, W, SEG_GM), (proj, W, SEG_GL), (o_mla, W, 0), (o_lru, W, 0)], [],
        [(W, BF16)] * 4)
    do = _mm("d_o", do_mla, wb["w_o_mla"], "nt")
    grads["w_o_mla"] = dw("dw_o_mla", o, do_mla)
    dqn, dqr_pre, *landed = _flash_dq(proj, qr, kv, kr, do, o, lse, ride=[grads[n] for n in early_grads])
    reduced = [_sum_slots("reduce_" + n, g) for n, g in zip(early_grads, landed)]
    dkv, dkr_h, *swapped = _flash_dkv(proj, qr, kv, kr, do, o, lse, ride=reduced)
    grads.update(zip(early_grads, swapped))
    dckvn = _mm("d_ckvn", dkv, wb["w_ukv"], "nt")
    grads["w_ukv"] = dw("dw_ukv", ckvn, dkv, owners=4)

    def rope_bwd_fn(dqr, dckvn, ckv, c, s1, s2, dkr_h, g):
        dkr = dkr_h[0]
        for h in range(1, N_HEADS):
            dkr = dkr + dkr_h[h]
        dckv, dg = _rms_bwd(ckv, g, dckvn)
        return _rope_heads(dqr, c, -s1, -s2), _rope_group(dkr, c, -s1, -s2), dckv, dg

    dqr, dkr, dckv, grads["g_ckv"] = _rope_bwd(rope_bwd_fn, t, tm, dqr_pre, dckvn, proj, cos_t, s1_t, s2_t, dkr_h, g_ckv)
    doli = _mm("d_oli", do_lru, wb["w_o_lru"], "nt")
    grads["w_o_lru"] = dw("dw_o_lru", oli, do_lru)
    (dlx, dly, grads["w_conv_lru"], grads["b_conv_lru"], grads["w_rg"], grads["b_rg"], grads["w_ig"],
     grads["b_ig"], grads["lru_lambda"]) = _lru_bwd(proj, hl, doli, w_conv, b_conv, w_rg, b_rg, w_ig, b_ig, lam)
    dproj = jnp.concatenate([dqn, dqr, dlx, dly, dgm, dgl, dckv, dkr,
                             jnp.zeros((t, PROJ_W - 12928), BF16)], axis=1)
    dw_in = _unperm_dw_in(_mm("dw_in", h1, dproj, "tn", out_dtype=BF16))
    grads["w_in"] = dw_in.reshape(W, 4, -1).transpose(1, 0, 2).reshape(8, W // 2, -1)
    dh1 = _mm("d_h1", dproj, w_in, "nt", ride=[grads[n] for n in late_grads])
    if late_grads:
        dh1, *landed = dh1
        grads.update(zip(late_grads, landed))

    def in_bwd_fn(dres, x, dh, g):
        dx, dg = _rms_bwd(x, g, dh)
        return dres + dx, dg

    grad_x, grads["g_pre_mix"] = _rows("bwd_pre_mix", in_bwd_fn, t, tm, [(dx1, W, 0), (x, W, 0), (dh1, W, 0)],
                                       [g_pre_mix], [(W, F32)], [(1, W)])
    return loss_acc[0, 0], grad_x, grads


def _rope_bwd(fn, t, tm, dqr_pre, dckvn, proj, cos_t, s1_t, s2_t, dkr_h, g_ckv):
    W = D_MODEL

    def body(dqr_ref, dck_ref, ckv_ref, c_ref, s1_ref, s2_ref, dkrh_ref, g_ref, o1, o2, o3, o4):
        r1, r2, r3, dg = fn(dqr_ref[...], dck_ref[...], ckv_ref[...], c_ref[...], s1_ref[...], s2_ref[...],
                            dkrh_ref[...], g_ref[...])
        o1[...] = r1.astype(o1.dtype)
        o2[...] = r2.astype(o2.dtype)
        o3[...] = r3.astype(o3.dtype)

        @pl.when(pl.program_id(0) == 0)
        def _():
            o4[...] = jnp.zeros_like(o4)

        o4[...] += dg

    rb = lambda w, cb=0: pl.BlockSpec((tm, w), lambda i, cb=cb: (i, cb))
    return pl.pallas_call(
        body, name="rope_bwd", grid=(t // tm,),
        in_specs=[rb(W), rb(512), rb(512, CKV_BLOCK), rb(LANES), rb(LANES), rb(LANES),
                  pl.BlockSpec((N_HEADS, tm, LANES), lambda i: (0, i, 0)),
                  pl.BlockSpec((1, 512), lambda i: (0, 0))],
        out_specs=[rb(W), rb(LANES), rb(512), pl.BlockSpec((1, 512), lambda i: (0, 0))],
        out_shape=[jax.ShapeDtypeStruct((t, W), BF16), jax.ShapeDtypeStruct((t, LANES), BF16),
                   jax.ShapeDtypeStruct((t, 512), BF16), jax.ShapeDtypeStruct((1, 512), F32)],
        compiler_params=_params(("arbitrary",)),
    )(dqr_pre, dckvn, proj, cos_t, s1_t, s2_t, dkr_h, g_ckv)


HBM = pl.BlockSpec(memory_space=pl.ANY)


def _me():
    return lax.axis_index("x"), lax.axis_index("y"), lax.axis_index("c")


def _gather_chips_ops(srcs, outs, send_sems, recv_sems, local_sems):
    n = len(srcs)
    x, y, c = _me()
    chips = [(1 - x, y), (x, 1 - y), (1 - x, 1 - y)]

    def copy(i, k, slot, to):
        return pltpu.make_async_remote_copy(
            src_ref=srcs[i], dst_ref=outs[i].at[slot], send_sem=send_sems.at[i, k], recv_sem=recv_sems.at[i, k],
            device_id=to, device_id_type=MESH)

    local = [pltpu.make_async_copy(srcs[i], outs[i].at[2 * x + y], local_sems.at[i]) for i in range(n)]
    sends = [copy(i, k, 2 * x + y, (px, py, c)) for i in range(n) for k, (px, py) in enumerate(chips)]

    def start():
        for cp in local + sends:
            cp.start()

    def finish():
        for i in range(n):
            for k, (px, py) in enumerate(chips):
                copy(i, k, 2 * px + py, (px, py, c)).wait_recv()
        for cp in sends:
            cp.wait_send()
        for cp in local:
            cp.wait()

    return start, finish


def _gather_sems(n):
    return [pltpu.SemaphoreType.DMA((n, 3)), pltpu.SemaphoreType.DMA((n, 3)), pltpu.SemaphoreType.DMA((n,))]


def _gather_chips(arrs):
    n = len(arrs)

    def body(*refs):
        start, finish = _gather_chips_ops(refs[:n], refs[n:2 * n], *refs[2 * n:])
        start()
        finish()

    return pl.pallas_call(
        body, name="gather_weights", in_specs=[HBM] * n, out_specs=[HBM] * n,
        out_shape=[jax.ShapeDtypeStruct((4,) + a.shape, a.dtype) for a in arrs],
        scratch_shapes=_gather_sems(n),
    )(*arrs)


def _exchange_all(name, arrs, scatter):
    n = len(arrs)

    def body(*refs):
        start, finish = _exchange_ops(refs[:n], refs[n:2 * n], *refs[2 * n:], scatter=scatter)
        start()
        finish()

    return pl.pallas_call(
        body, name=name, in_specs=[HBM] * n, out_specs=[HBM] * n,
        out_shape=_exchange_shapes(arrs, scatter), scratch_shapes=_exchange_sems(n),
    )(*arrs)


def _exchange_shapes(arrs, scatter):
    return [jax.ShapeDtypeStruct((8,) + (a.shape[1:] if scatter else a.shape), a.dtype) for a in arrs]


def _exchange_sems(n):
    return [pltpu.SemaphoreType.DMA((n, 8)), pltpu.SemaphoreType.DMA((n, 8)), pltpu.SemaphoreType.DMA((n,))]


def _exchange_ops(srcs, outs, send_sems, recv_sems, local_sems, scatter):
    n = len(srcs)
    x, y, c = _me()
    me = 4 * x + 2 * y + c

    def copy(i, q):
        return pltpu.make_async_remote_copy(
            src_ref=srcs[i].at[q] if scatter else srcs[i], dst_ref=outs[i].at[me],
            send_sem=send_sems.at[i, q], recv_sem=recv_sems.at[i, me],
            device_id=(q // 4, (q // 2) % 2, q % 2), device_id_type=MESH)

    def arrival(i, s):
        return pltpu.make_async_remote_copy(
            src_ref=srcs[i].at[s] if scatter else srcs[i], dst_ref=outs[i].at[s],
            send_sem=send_sems.at[i, s], recv_sem=recv_sems.at[i, s],
            device_id=(s // 4, (s // 2) % 2, s % 2), device_id_type=MESH)

    local = [pltpu.make_async_copy(srcs[i].at[me] if scatter else srcs[i], outs[i].at[me], local_sems.at[i])
             for i in range(n)]

    def start():
        for cp in local:
            cp.start()
        for q in range(8):
            @pl.when(me != q)
            def _(q=q):
                for i in range(n):
                    copy(i, q).start()

    def finish():
        for s in range(8):
            @pl.when(me != s)
            def _(s=s):
                for i in range(n):
                    arrival(i, s).wait_recv()
        for q in range(8):
            @pl.when(me != q)
            def _(q=q):
                for i in range(n):
                    copy(i, q).wait_send()
        for cp in local:
            cp.wait()

    return start, finish


D2D_CHUNK_BYTES = 2 * 1024 * 1024


def _core_chunks(arrs):
    chunks = []
    for i, a in enumerate(arrs):
        r = a.shape[0]
        k = 1
        while a.size * a.dtype.itemsize // k > D2D_CHUNK_BYTES and r % (2 * k) == 0 and (r // (2 * k)) % SUBLANES == 0:
            k *= 2
        chunks += [(i, j * (r // k), r // k) for j in range(k)]
    return chunks


def _gather_cores_ops(srcs, outs, send_sems, recv_sems, local_sems, chunks):
    x, y, c = _me()

    def copy(j, slot):
        i, r0, nr = chunks[j]
        return pltpu.make_async_remote_copy(
            src_ref=srcs[i].at[pl.ds(r0, nr)], dst_ref=outs[i].at[slot, pl.ds(r0, nr)],
            send_sem=send_sems.at[j], recv_sem=recv_sems.at[j],
            device_id=(x, y, 1 - c), device_id_type=MESH)

    local = [pltpu.make_async_copy(srcs[i].at[pl.ds(r0, nr)], outs[i].at[c, pl.ds(r0, nr)], local_sems.at[j])
             for j, (i, r0, nr) in enumerate(chunks)]
    sends = [copy(j, c) for j in range(len(chunks))]

    def start():
        for cp in sends + local:
            cp.start()

    def finish():
        for j in range(len(chunks)):
            copy(j, 1 - c).wait_recv()
        for cp in sends:
            cp.wait_send()
        for cp in local:
            cp.wait()

    return start, finish


def _core_shapes(arrs):
    return [jax.ShapeDtypeStruct((2,) + a.shape, a.dtype) for a in arrs]


def _core_sems(chunks):
    return [pltpu.SemaphoreType.DMA((len(chunks),))] * 3


def _gather_cores(arrs):
    n = len(arrs)
    chunks = _core_chunks(arrs)

    def body(*refs):
        start, finish = _gather_cores_ops(refs[:n], refs[n:2 * n], *refs[2 * n:], chunks=chunks)
        start()
        finish()

    return pl.pallas_call(
        body, name="gather_cores", in_specs=[HBM] * n, out_specs=[HBM] * n,
        out_shape=_core_shapes(arrs), scratch_shapes=_core_sems(chunks),
    )(*arrs)


def _row_tile(rows, row_bytes, budget):
    for t in (2048, 1024, 512, 256, 128, 64, 32, 16, 8):
        if rows % t == 0 and t * row_bytes <= budget:
            return t
    return rows


def _sum_slots(name, buf):
    _, r, w = buf.shape
    tr = _row_tile(r, 8 * w * buf.dtype.itemsize, 8 * 1024 * 1024)

    def body(b_ref, o_ref):
        acc = b_ref[0].astype(F32)
        for s in range(1, 8):
            acc = acc + b_ref[s].astype(F32)
        o_ref[...] = acc

    return pl.pallas_call(
        body, name=name, grid=(r // tr,),
        in_specs=[pl.BlockSpec((8, tr, w), lambda i: (0, i, 0))],
        out_specs=pl.BlockSpec((tr, w), lambda i: (i, 0)),
        out_shape=jax.ShapeDtypeStruct((r, w), F32),
        compiler_params=_params(("parallel",)),
    )(buf)


def _adamw(name, w, g, m, v, tr):
    def fn(w, g, m, v):
        m = ADAM_B1 * m + (1.0 - ADAM_B1) * g
        v = ADAM_B2 * v + (1.0 - ADAM_B2) * (g * g)
        m_hat = m / (1.0 - ADAM_B1 ** ADAM_STEP)
        v_hat = v / (1.0 - ADAM_B2 ** ADAM_STEP)
        delta = -ADAM_LR * (m_hat / (jnp.sqrt(v_hat) + ADAM_EPS) + ADAM_WD * w)
        return delta, m, v

    r, c = w.shape
    return _rows(name, fn, r, tr, [(a, c, 0) for a in (w, g, m, v)], [], [(c, F32)] * 3)


def _pack(arrays, n_rows, dtype):
    flat = jnp.concatenate([a.reshape(-1).astype(dtype) for a in arrays])
    return jnp.pad(flat, (0, n_rows * FLAT_W - flat.shape[0])).reshape(n_rows, FLAT_W)


def _unpack(flat, shapes):
    flat = flat.reshape(-1)
    out, off = [], 0
    for s in shapes:
        n = math.prod(s)
        out.append(flat[off:off + n].reshape(s))
        off += n
    return out


def kernel(x, mem, positions, g_pre_mix, g_post_mix, w_in, g_ckv, w_ukv, w_o_mla, w_conv_lru, b_conv_lru, w_rg, b_rg, w_ig, b_ig, lru_lambda, w_o_lru, w_out, g_pre_x, g_post_x, g_mem, w_cq, w_ck, w_cv, w_co, g_pre_ffn, g_post_ffn, w_up, w_fconv, b_fconv, w_down, loss_target, m_g_pre_mix, m_g_post_mix, m_w_in, m_g_ckv, m_w_ukv, m_w_o_mla, m_w_conv_lru, m_b_conv_lru, m_w_rg, m_b_rg, m_w_ig, m_b_ig, m_lru_lambda, m_w_o_lru, m_w_out, m_g_pre_x, m_g_post_x, m_g_mem, m_w_cq, m_w_ck, m_w_cv, m_w_co, m_g_pre_ffn, m_g_post_ffn, m_w_up, m_w_fconv, m_b_fconv, m_w_down, v_g_pre_mix, v_g_post_mix, v_w_in, v_g_ckv, v_w_ukv, v_w_o_mla, v_w_conv_lru, v_b_conv_lru, v_w_rg, v_b_rg, v_w_ig, v_b_ig, v_lru_lambda, v_w_o_lru, v_w_out, v_g_pre_x, v_g_post_x, v_g_mem, v_w_cq, v_w_ck, v_w_cv, v_w_co, v_g_pre_ffn, v_g_post_ffn, v_w_up, v_w_fconv, v_b_fconv, v_w_down):
    given = dict(locals())
    small_names = [n for n, _ in SMALL]
    conv_names = [n for n, _, _ in CONV]

    mine = {n: given[n][0].astype(BF16) for n, _, _ in BIG}
    mine.update({n: given[n][0] for n in conv_names})
    first = [d for d in BIG if d[0] in FIRST_WEIGHTS]
    late = [(d, mine[d[0]]) for d in BIG + CONV if d[0] not in FIRST_WEIGHTS]
    wb = _whole(first, _gather_chips([mine[n] for n, _, _ in first]))
    ws = {n: given[n][0] for n in small_names}

    late_names = tuple(n for n, _, _ in BIG if n not in EARLY_GRADS)
    loss_part, grad_x, grads = _local_step(x[0], mem[0], positions[0], loss_target[0], wb, ws,
                                           late_weights=late, early_grads=EARLY_GRADS, late_grads=late_names)
    loss = lax.psum(loss_part, ("x", "y", "c"))
    grads.update(zip(late_names, _gather_cores([_sum_slots("reduce_" + n, grads[n]) for n in late_names])))
    g_big = {n: grads[n].reshape(s) for n, s, _ in BIG}

    xs_shapes = [s for _, s in SMALL] + [(s[0], 4 * s[1]) for _, s, _ in CONV]
    s_part = _pack([grads[n].reshape(s) for n, s in zip(small_names + conv_names, xs_shapes)], SMALL_XCHG_ROWS, F32)
    (s_all,) = _exchange_all("gather_small", [s_part], scatter=False)
    g_small = dict(zip(small_names + conv_names, _unpack(_sum_slots("reduce_small", s_all), xs_shapes)))
    chip = 2 * lax.axis_index("x") + lax.axis_index("y")
    for n, s, _ in CONV:
        g_small[n] = lax.dynamic_slice_in_dim(g_small[n], chip * s[1], s[1], axis=1)

    out = {n: (g_big[n],) + tuple(_adamw("adamw_" + n, given[n][0], g_big[n], given["m_" + n][0], given["v_" + n][0],
                                         _row_tile(s[0], 4 * s[1], 1536 * 1024))) for n, s, _ in BIG}
    sm_shapes = [s for _, s in SMALL] + [s for _, s, _ in CONV]
    pack_small = lambda pre: _pack([given[pre + n][0] for n in small_names + conv_names], SMALL_ROWS, F32)
    g_flat = _pack([g_small[n] for n in small_names + conv_names], SMALL_ROWS, F32)
    small_out = (g_flat,) + tuple(_adamw("adamw_small", pack_small(""), g_flat, pack_small("m_"), pack_small("v_"), 280))
    for kind in range(4):
        for n, a in zip(small_names + conv_names, _unpack(small_out[kind], sm_shapes)):
            out.setdefault(n, [None] * 4)
            out[n] = list(out[n])
            out[n][kind] = a
    outs = [out[n][kind][None] for kind in range(4) for n in WEIGHTS]
    return (loss, grad_x[None], *outs)
```
